```python
import math
import jax, jax.numpy as jnp
from jax import lax
import numpy as np


D_MODEL = 1024
BATCH = 4
SEQ = 8192
DEPTH = 4

RMS_EPS = 1e-6
Q_BLOCK = 128
NEG_INF = -1e30

T5_BUCKETS = 32
T5_MAX_EXACT = T5_BUCKETS // 2
T5_MAX_DISTANCE = 1024

DIFF_HEADS = 4
DIFF_HEAD_DIM = 64
DIFF_QK_WIDTH = DIFF_HEADS * 2 * DIFF_HEAD_DIM
DIFF_V_WIDTH = DIFF_HEADS * 2 * DIFF_HEAD_DIM

MOBA_HEADS = 4
MOBA_HEAD_DIM = 128
MOBA_BLOCK = 256
MOBA_TOPK = 3
MOBA_Q_CHUNK = 32
MOBA_WIDTH = MOBA_HEADS * MOBA_HEAD_DIM

MLA_HEADS = 8
MLA_Q_RANK = 256
MLA_KV_RANK = 128
MLA_NOPE_DIM = 64
MLA_ROPE_DIM = 32
MLA_V_DIM = 64
MLA_WIDTH = MLA_HEADS * MLA_V_DIM
ROPE_THETA = 10000.0

N_BRANCH = 3
BRANCH_WIDTH = 512
N_BIAS_HEADS = DIFF_HEADS + MOBA_HEADS
D_FF = 4 * D_MODEL

IN_SPLITS = (DIFF_QK_WIDTH, DIFF_QK_WIDTH, DIFF_V_WIDTH,
             MOBA_WIDTH, MOBA_WIDTH, MOBA_WIDTH,
             MLA_Q_RANK, MLA_KV_RANK, MLA_ROPE_DIM,
             N_BRANCH * D_MODEL)
IN_OFFSETS = tuple(sum(IN_SPLITS[:i + 1]) for i in range(len(IN_SPLITS) - 1))
D_IN = sum(IN_SPLITS)

kernel_name = 'hybrid_diff_moba_mla_gated_trunk'


def _rms_norm(x, w):
    xf = x.astype(jnp.float32)
    y = xf * lax.rsqrt(jnp.mean(xf * xf, axis=-1, keepdims=True) + RMS_EPS)
    return (y * w.astype(jnp.float32)).astype(x.dtype)


def _t5_bucket(rel):
    n = jnp.maximum(rel, 0)
    nf = jnp.maximum(n, 1).astype(jnp.float32)
    large = T5_MAX_EXACT + (jnp.log(nf / T5_MAX_EXACT)
                            / math.log(T5_MAX_DISTANCE / T5_MAX_EXACT)
                            * (T5_BUCKETS - T5_MAX_EXACT)).astype(jnp.int32)
    large = jnp.minimum(large, T5_BUCKETS - 1)
    return jnp.where(n < T5_MAX_EXACT, n, large)


def _rope(x, positions):
    d = x.shape[-1]
    half = d // 2
    inv_freq = ROPE_THETA ** (-jnp.arange(half, dtype=jnp.float32) * 2.0 / d)
    ang = positions.astype(jnp.float32)[:, :, None, None] * inv_freq
    cos, sin = jnp.cos(ang), jnp.sin(ang)
    xf = x.astype(jnp.float32)
    x1, x2 = xf[..., :half], xf[..., half:]
    return jnp.concatenate([x1 * cos - x2 * sin, x2 * cos + x1 * sin], axis=-1).astype(x.dtype)


def _diff_attention(q1, q2, k1, k2, v, lam, bias_table):
    B, S, H, dh = q1.shape
    scale = 1.0 / math.sqrt(dh)
    k_pos = jnp.arange(S)
    table = bias_table.astype(jnp.float32)

    def block(c):
        start = c * Q_BLOCK
        q_pos = start + jnp.arange(Q_BLOCK)
        rel = q_pos[:, None] - k_pos[None, :]
        bias = jnp.transpose(table[_t5_bucket(rel)], (2, 0, 1))
        visible = rel >= 0

        def probs(q, k):
            qb = lax.dynamic_slice_in_dim(q, start, Q_BLOCK, axis=1)
            s = jnp.einsum('bqhd,bkhd->bhqk', qb, k).astype(jnp.float32) * scale + bias
            return jax.nn.softmax(jnp.where(visible, s, NEG_INF), axis=-1)

        w = probs(q1, k1) - lam * probs(q2, k2)
        return jnp.einsum('bhqk,bkhd->bqhd', w.astype(v.dtype), v)

    out = lax.map(block, jnp.arange(S // Q_BLOCK))
    return jnp.transpose(out, (1, 0, 2, 3, 4)).reshape(B, S, H, v.shape[-1])


def _moba_attention(q, k, v, bias_table):
    B, S, H, d = q.shape
    scale = 1.0 / math.sqrt(d)
    n_blk = -(-S // MOBA_BLOCK)
    s_pad = n_blk * MOBA_BLOCK
    topk = min(MOBA_TOPK, n_blk)
    pad = ((0, 0), (0, s_pad - S), (0, 0), (0, 0))
    kbh = jnp.transpose(jnp.pad(k, pad).reshape(B, n_blk, MOBA_BLOCK, H, d), (0, 3, 1, 2, 4))
    vbh = jnp.transpose(jnp.pad(v, pad).reshape(B, n_blk, MOBA_BLOCK, H, d), (0, 3, 1, 2, 4))
    k_mean = jnp.mean(kbh.astype(jnp.float32), axis=3)
    blk_ids = jnp.arange(n_blk)
    offs = jnp.arange(MOBA_BLOCK)
    table = bias_table.astype(jnp.float32)
    table_t = table.T
    bi = jnp.arange(B)[:, None, None, None]
    hi = jnp.arange(H)[None, :, None, None]
    hi5 = jnp.arange(H)[None, :, None, None, None]

    def chunk(c):
        start = c * MOBA_Q_CHUNK
        own = start // MOBA_BLOCK
        q_pos = start + jnp.arange(MOBA_Q_CHUNK)
        qbt = jnp.transpose(lax.dynamic_slice_in_dim(q, start, MOBA_Q_CHUNK, axis=1), (0, 2, 1, 3))
        gate = jnp.einsum('bhqd,bhnd->bhqn', qbt.astype(jnp.float32), k_mean)
        gate = jnp.where(blk_ids < own, gate, NEG_INF)
        _, idx = lax.top_k(gate, topk)
        valid = idx < own
        k_sel = kbh[bi, hi, idx]
        v_sel = vbh[bi, hi, idx]
        s_sel = jnp.einsum('bhqd,bhqnld->bhqnl', qbt, k_sel).astype(jnp.float32) * scale
        sel_pos = idx[..., None] * MOBA_BLOCK + offs
        rel_sel = q_pos[None, None, :, None, None] - sel_pos
        s_sel = jnp.where(valid[..., None], s_sel + table_t[hi5, _t5_bucket(rel_sel)], NEG_INF)
        k_own = lax.dynamic_index_in_dim(kbh, own, axis=2, keepdims=False)
        v_own = lax.dynamic_index_in_dim(vbh, own, axis=2, keepdims=False)
        rel_own = q_pos[:, None] - (own * MOBA_BLOCK + offs)[None, :]
        bias_own = jnp.transpose(table[_t5_bucket(rel_own)], (2, 0, 1))
        s_own = jnp.einsum('bhqd,bhld->bhql', qbt, k_own).astype(jnp.float32) * scale + bias_own
        s_own = jnp.where(rel_own >= 0, s_own, NEG_INF)
        n_sel = topk * MOBA_BLOCK
        s = jnp.concatenate([s_sel.reshape(B, H, MOBA_Q_CHUNK, n_sel), s_own], axis=-1)
        p = jax.nn.softmax(s, axis=-1).astype(v.dtype)
        p_sel = p[..., :n_sel].reshape(B, H, MOBA_Q_CHUNK, topk, MOBA_BLOCK)
        p_own = p[..., n_sel:]
        return (jnp.einsum('bhqnl,bhqnld->bqhd', p_sel, v_sel)
                + jnp.einsum('bhql,bhld->bqhd', p_own, v_own))

    out = lax.map(chunk, jnp.arange(S // MOBA_Q_CHUNK))
    return jnp.transpose(out, (1, 0, 2, 3, 4)).reshape(B, S, H, d)


def _dense_causal_attention(q, k, v, scale):
    B, S, H, _ = q.shape
    k_pos = jnp.arange(S)

    def block(c):
        start = c * Q_BLOCK
        qb = lax.dynamic_slice_in_dim(q, start, Q_BLOCK, axis=1)
        s = jnp.einsum('bqhd,bkhd->bhqk', qb, k).astype(jnp.float32) * scale
        rel = (start + jnp.arange(Q_BLOCK))[:, None] - k_pos[None, :]
        p = jax.nn.softmax(jnp.where(rel >= 0, s, NEG_INF), axis=-1).astype(v.dtype)
        return jnp.einsum('bhqk,bkhd->bqhd', p, v)

    out = lax.map(block, jnp.arange(S // Q_BLOCK))
    return jnp.transpose(out, (1, 0, 2, 3, 4)).reshape(B, S, H, v.shape[-1])


def setup_inputs(seed: int = 0) -> dict:
    key = jax.random.key(seed)
    ks = jax.random.split(key, 20)

    def nrm(k, shape, scale):
        return jax.random.normal(k, shape, jnp.float32) * scale

    x = nrm(ks[0], (BATCH, SEQ, D_MODEL), 1.0)
    offsets = jax.random.randint(ks[1], (BATCH, 1), 0, 4096, dtype=jnp.int32)
    positions = offsets + jnp.arange(SEQ, dtype=jnp.int32)[None, :]
    return {
        'x': x,
        'positions': positions,
        'rel_bias': nrm(ks[2], (T5_BUCKETS, N_BIAS_HEADS), 0.5),
        'norm_mix_pre': 1.0 + nrm(ks[3], (DEPTH, D_MODEL), 0.05),
        'norm_mix_post': 1.0 + nrm(ks[4], (DEPTH, D_MODEL), 0.05),
        'norm_mlp_pre': 1.0 + nrm(ks[5], (DEPTH, D_MODEL), 0.05),
        'norm_mlp_post': 1.0 + nrm(ks[6], (DEPTH, D_MODEL), 0.05),
        'w_in': nrm(ks[7], (DEPTH, D_MODEL, D_IN), D_MODEL ** -0.5),
        'diff_lambda': nrm(ks[8], (DEPTH, 4, DIFF_HEAD_DIM), 0.1),
        'diff_subln': 1.0 + nrm(ks[9], (DEPTH, 2 * DIFF_HEAD_DIM), 0.05),
        'mla_q_norm': 1.0 + nrm(ks[10], (DEPTH, MLA_Q_RANK), 0.05),
        'mla_w_uq': nrm(ks[11], (DEPTH, MLA_Q_RANK, MLA_HEADS * (MLA_NOPE_DIM + MLA_ROPE_DIM)), MLA_Q_RANK ** -0.5),
        'mla_kv_norm': 1.0 + nrm(ks[12], (DEPTH, MLA_KV_RANK), 0.05),
        'mla_w_ukv': nrm(ks[13], (DEPTH, MLA_KV_RANK, MLA_HEADS * (MLA_NOPE_DIM + MLA_V_DIM)), MLA_KV_RANK ** -0.5),
        'w_branch': nrm(ks[14], (DEPTH, N_BRANCH, BRANCH_WIDTH, D_MODEL), BRANCH_WIDTH ** -0.5),
        'w_out': nrm(ks[15], (DEPTH, D_MODEL, D_MODEL), D_MODEL ** -0.5),
        'w_up': nrm(ks[16], (DEPTH, D_MODEL, D_FF), D_MODEL ** -0.5),
        'w_down': nrm(ks[17], (DEPTH, D_FF, D_MODEL), D_FF ** -0.5),
    }


def reference(x, positions, rel_bias, norm_mix_pre, norm_mix_post, norm_mlp_pre, norm_mlp_post,
              w_in, diff_lambda, diff_subln, mla_q_norm, mla_w_uq, mla_kv_norm, mla_w_ukv,
              w_branch, w_out, w_up, w_down):
    B, S, _ = x.shape
    for l in range(DEPTH):
        lam_init = 0.8 - 0.6 * math.exp(-0.3 * l)
        h = _rms_norm(x, norm_mix_pre[l])
        proj = jnp.einsum('bsd,de->bse', h, w_in[l])
        dq, dk, dv, mq, mk, mv, c_q, c_kv, k_pe, g = jnp.split(proj, IN_OFFSETS, axis=-1)

        dq = dq.reshape(B, S, DIFF_HEADS, 2, DIFF_HEAD_DIM)
        dk = dk.reshape(B, S, DIFF_HEADS, 2, DIFF_HEAD_DIM)
        dv = dv.reshape(B, S, DIFF_HEADS, 2 * DIFF_HEAD_DIM)
        lam_vecs = diff_lambda[l].astype(jnp.float32)
        lam = (jnp.exp(jnp.sum(lam_vecs[0] * lam_vecs[1]))
               - jnp.exp(jnp.sum(lam_vecs[2] * lam_vecs[3])) + lam_init)
        oa = _diff_attention(dq[..., 0, :], dq[..., 1, :], dk[..., 0, :], dk[..., 1, :], dv,
                             lam, rel_bias[:, :DIFF_HEADS])
        oa = (_rms_norm(oa, diff_subln[l]) * (1.0 - lam_init)).reshape(B, S, DIFF_V_WIDTH)

        ob = _moba_attention(mq.reshape(B, S, MOBA_HEADS, MOBA_HEAD_DIM),
                             mk.reshape(B, S, MOBA_HEADS, MOBA_HEAD_DIM),
                             mv.reshape(B, S, MOBA_HEADS, MOBA_HEAD_DIM),
                             rel_bias[:, DIFF_HEADS:]).reshape(B, S, MOBA_WIDTH)

        q = jnp.einsum('bsr,re->bse', _rms_norm(c_q, mla_q_norm[l]), mla_w_uq[l])
        q = q.reshape(B, S, MLA_HEADS, MLA_NOPE_DIM + MLA_ROPE_DIM)
        q = jnp.concatenate([q[..., :MLA_NOPE_DIM], _rope(q[..., MLA_NOPE_DIM:], positions)], axis=-1)
        kv = jnp.einsum('bsr,re->bse', _rms_norm(c_kv, mla_kv_norm[l]), mla_w_ukv[l])
        kv = kv.reshape(B, S, MLA_HEADS, MLA_NOPE_DIM + MLA_V_DIM)
        k_rope = jnp.broadcast_to(_rope(k_pe[:, :, None, :], positions), (B, S, MLA_HEADS, MLA_ROPE_DIM))
        k = jnp.concatenate([kv[..., :MLA_NOPE_DIM], k_rope], axis=-1)
        oc = _dense_causal_attention(q, k, kv[..., MLA_NOPE_DIM:],
                                     1.0 / math.sqrt(MLA_NOPE_DIM + MLA_ROPE_DIM)).reshape(B, S, MLA_WIDTH)

        branches = jnp.stack([oa, ob, oc], axis=2)
        br = jnp.einsum('bsgc,gcd->bsgd', branches, w_branch[l])
        gates = jax.nn.sigmoid(g.reshape(B, S, N_BRANCH, D_MODEL))
        mixed = jnp.einsum('bsd,de->bse', jnp.sum(gates * br, axis=2), w_out[l])
        x = x + _rms_norm(mixed, norm_mix_post[l])

        h = _rms_norm(x, norm_mlp_pre[l])
        u = jnp.square(jax.nn.relu(jnp.einsum('bsd,df->bsf', h, w_up[l])))
        x = x + _rms_norm(jnp.einsum('bsf,fd->bsd', u, w_down[l]), norm_mlp_post[l])
    return x
```

```python
import functools
import math

import jax
import jax.numpy as jnp
from jax import lax
from jax.experimental import pallas as pl
from jax.experimental.pallas import tpu as pltpu

F32 = jnp.float32
BF16 = jnp.bfloat16

RMS_EPS = 1e-6
NEG_INF = -1e30

T5_BUCKETS = 32
T5_MAX_EXACT = T5_BUCKETS // 2
T5_MAX_DISTANCE = 1024

DIFF_HEADS = 4
DIFF_HEAD_DIM = 64
MOBA_HEADS = 4
MOBA_HEAD_DIM = 128
MOBA_BLOCK = 256
MOBA_TOPK = 3
MLA_HEADS = 8
MLA_Q_RANK = 256
MLA_KV_RANK = 128
MLA_NOPE_DIM = 64
MLA_ROPE_DIM = 32
MLA_V_DIM = 64
ROPE_THETA = 10000.0
N_BRANCH = 3
BRANCH_WIDTH = 512

LANES = 128
ATTN_TILE = 256
VMEM_LIMIT = 48 * 1024 * 1024


def _cparams(*sem):
    return pltpu.CompilerParams(dimension_semantics=sem, vmem_limit_bytes=VMEM_LIMIT)


def _rms(x, w):
    return x * lax.rsqrt(jnp.mean(x * x, axis=-1, keepdims=True) + RMS_EPS) * w


def _dot(a, b):
    return jnp.dot(a, b, preferred_element_type=F32)


def _dot_nt(a, b):
    return lax.dot_general(a, b, (((1,), (1,)), ((), ())), preferred_element_type=F32)


def _norm_matmul_kernel(x_ref, nw_ref, w_ref, cs_ref, o_ref, h_ref):
    @pl.when(pl.program_id(1) == 0)
    def _():
        h_ref[...] = _rms(x_ref[...], nw_ref[...]).astype(BF16)

    o_ref[...] = (_dot(h_ref[...], w_ref[...]) * cs_ref[...]).astype(o_ref.dtype)


def _norm_matmul(x, nw, w, colscale, out_dtype, tm, tn):
    n, d = x.shape
    nout = w.shape[1]
    return pl.pallas_call(
        _norm_matmul_kernel,
        grid=(n // tm, nout // tn),
        in_specs=[
            pl.BlockSpec((tm, d), lambda i, j: (i, 0)),
            pl.BlockSpec((1, d), lambda i, j: (0, 0)),
            pl.BlockSpec((d, tn), lambda i, j: (0, j)),
            pl.BlockSpec((1, tn), lambda i, j: (0, j)),
        ],
        out_specs=pl.BlockSpec((tm, tn), lambda i, j: (i, j)),
        out_shape=jax.ShapeDtypeStruct((n, nout), out_dtype),
        scratch_shapes=[pltpu.VMEM((tm, d), BF16)],
        compiler_params=_cparams("parallel", "arbitrary"),
        name="norm_matmul",
    )(x, nw, w, colscale)


def _bias_tile_kernel(tab_ref, o_ref, *, tile):
    h = pl.program_id(0)
    d = pl.program_id(1)
    r = lax.broadcasted_iota(jnp.int32, (tile, tile), 0)
    c = lax.broadcasted_iota(jnp.int32, (tile, tile), 1)
    rel = d * tile + r - c
    n = jnp.maximum(rel, 0)
    nf = jnp.maximum(n, 1).astype(F32)
    large = T5_MAX_EXACT + (jnp.log(nf / T5_MAX_EXACT)
                            / math.log(T5_MAX_DISTANCE / T5_MAX_EXACT)
                            * (T5_BUCKETS - T5_MAX_EXACT)).astype(jnp.int32)
    large = jnp.minimum(large, T5_BUCKETS - 1)
    bucket = jnp.where(n < T5_MAX_EXACT, n, large)
    val = jnp.zeros((tile, tile), F32)
    for b in range(T5_BUCKETS):
        val = jnp.where(bucket == b, tab_ref[b, h], val)
    o_ref[0, 0] = jnp.where(rel >= 0, val, NEG_INF)


def _num_near_tiles(tile):
    return -(-(T5_MAX_DISTANCE + tile - 1) // tile)


def _bias_tiles(rel_bias, tile):
    nh = rel_bias.shape[1]
    nd = _num_near_tiles(tile)
    return pl.pallas_call(
        functools.partial(_bias_tile_kernel, tile=tile),
        grid=(nh, nd),
        in_specs=[pl.BlockSpec(memory_space=pltpu.SMEM)],
        out_specs=pl.BlockSpec((1, 1, tile, tile), lambda h, d: (h, d, 0, 0)),
        out_shape=jax.ShapeDtypeStruct((nh, nd, tile, tile), F32),
        compiler_params=_cparams("parallel", "parallel"),
        name="t5_bias_tiles",
    )(rel_bias)


def _softmax_update(s, v, m_ref, l_ref, acc_ref):
    m_prev = m_ref[...]
    m_new = jnp.maximum(m_prev, jnp.max(s, axis=1, keepdims=True))
    alpha = jnp.exp(m_prev - m_new)
    p = jnp.exp(s - m_new)
    l_ref[...] = alpha * l_ref[...] + jnp.sum(p, axis=1, keepdims=True)
    acc_ref[...] = alpha * acc_ref[...] + _dot(p.astype(BF16), v)
    m_ref[...] = m_new


def _init_stats(m_ref, l_ref, acc_ref):
    m_ref[...] = jnp.full(m_ref.shape, NEG_INF, F32)
    l_ref[...] = jnp.zeros(l_ref.shape, F32)
    acc_ref[...] = jnp.zeros(acc_ref.shape, F32)


def _diff_kernel(tab_ref, scal_ref, lam_ref, subw_ref, q_ref, k_ref, v_ref, bias_ref, o_ref,
                 m1, l1, a1, m2, l2, a2, *, tile, nd):
    h = pl.program_id(1)
    i = pl.program_id(2)
    q = q_ref[0]
    lane = lax.broadcasted_iota(jnp.int32, q.shape, 1)
    zero = jnp.zeros_like(q)
    q1 = jnp.where(lane < DIFF_HEAD_DIM, q, zero)
    q2 = jnp.where(lane >= DIFF_HEAD_DIM, q, zero)
    _init_stats(m1, l1, a1)
    _init_stats(m2, l2, a2)
    far_bias = tab_ref[T5_BUCKETS - 1, h]

    def step(j, bias):
        k = k_ref[0, pl.ds(pl.multiple_of(j * tile, tile), tile), :]
        v = v_ref[0, pl.ds(pl.multiple_of(j * tile, tile), tile), :]
        _softmax_update(_dot_nt(q1, k) + bias, v, m1, l1, a1)
        _softmax_update(_dot_nt(q2, k) + bias, v, m2, l2, a2)

    n_far = jnp.maximum(i - (nd - 1), 0)

    def far_body(j, c):
        step(j, far_bias)
        return c

    def near_body(j, c):
        step(j, bias_ref[0, i - j])
        return c

    lax.fori_loop(0, n_far, far_body, 0)
    lax.fori_loop(n_far, i + 1, near_body, 0)

    lam_init = scal_ref[0]
    lv = lam_ref[...]
    lam = (jnp.exp(jnp.sum(lv[0:1] * lv[1:2], axis=1, keepdims=True))
           - jnp.exp(jnp.sum(lv[2:3] * lv[3:4], axis=1, keepdims=True)) + lam_init)
    o = a1[...] / l1[...] - lam * (a2[...] / l2[...])
    o_ref[0] = (_rms(o, subw_ref[...]) * (1.0 - lam_init)).astype(o_ref.dtype)


def _diff_attention(qkv, bias, rel_bias, scal, lam_vecs, subw, tile):
    b, s, _ = qkv.shape
    nd = bias.shape[1]
    hd = 2 * DIFF_HEAD_DIM
    return pl.pallas_call(
        functools.partial(_diff_kernel, tile=tile, nd=nd),
        grid=(b, DIFF_HEADS, s // tile),
        in_specs=[
            pl.BlockSpec(memory_space=pltpu.SMEM),
            pl.BlockSpec(memory_space=pltpu.SMEM),
            pl.BlockSpec(lam_vecs.shape, lambda bi, h, i: (0, 0)),
            pl.BlockSpec((1, hd), lambda bi, h, i: (0, 0)),
            pl.BlockSpec((1, tile, hd), lambda bi, h, i: (bi, i, h)),
            pl.BlockSpec((1, s, hd), lambda bi, h, i: (bi, 0, DIFF_HEADS + h)),
            pl.BlockSpec((1, s, hd), lambda bi, h, i: (bi, 0, 2 * DIFF_HEADS + h)),
            pl.BlockSpec((1, nd, tile, tile), lambda bi, h, i: (h, 0, 0, 0)),
        ],
        out_specs=pl.BlockSpec((1, tile, hd), lambda bi, h, i: (bi, i, h)),
        out_shape=jax.ShapeDtypeStruct((b, s, DIFF_HEADS * hd), BF16),
        scratch_shapes=[pltpu.VMEM((tile, 1), F32), pltpu.VMEM((tile, 1), F32), pltpu.VMEM((tile, hd), F32),
                        pltpu.VMEM((tile, 1), F32), pltpu.VMEM((tile, 1), F32), pltpu.VMEM((tile, hd), F32)],
        compiler_params=_cparams("parallel", "parallel", "arbitrary"),
        name="diff_attention",
    )(rel_bias, scal, lam_vecs, subw, qkv, qkv, qkv, bias)


def _moba_kernel(tab_ref, q_ref, k_ref, v_ref, bias_ref, o_ref,
                 kmean_ref, qaug_ref, m, l, acc, *, nd, nblk, head0):
    blk = MOBA_BLOCK
    h = pl.program_id(1)
    i = pl.program_id(2)

    @pl.when(i == 0)
    def _():
        kmean_ref[...] = jnp.zeros(kmean_ref.shape, F32)
        for n in range(nblk):
            kb = k_ref[0, n * blk:(n + 1) * blk, :].astype(F32)
            kmean_ref[n:n + 1, :] = jnp.mean(kb, axis=0, keepdims=True)

    q = q_ref[0]
    gate = lax.dot_general(q.astype(F32), kmean_ref[...], (((1,), (1,)), ((), ())),
                           precision=lax.Precision.HIGHEST, preferred_element_type=F32)
    lane = lax.broadcasted_iota(jnp.int32, gate.shape, 1)
    lane_f = lane.astype(F32)
    valid = lane < i
    gate = jnp.where(valid, gate, NEG_INF)
    sel = jnp.zeros(gate.shape, jnp.bool_)
    for _ in range(MOBA_TOPK):
        top = jnp.max(gate, axis=1, keepdims=True)
        first = jnp.min(jnp.where(gate == top, lane_f, float(LANES)), axis=1, keepdims=True)
        hit = lane_f == first
        sel = jnp.logical_or(sel, jnp.logical_and(hit, valid))
        gate = jnp.where(hit, -3.0e38, gate)
    qaug_ref[:, :LANES] = q
    qaug_ref[:, LANES:] = jnp.where(sel, 0.0, NEG_INF).astype(BF16)

    _init_stats(m, l, acc)
    own = pl.ds(pl.multiple_of(i * blk, blk), blk)
    _softmax_update(_dot_nt(q, k_ref[0, own, :]) + bias_ref[0, 0], v_ref[0, own, :], m, l, acc)

    far_bias = tab_ref[T5_BUCKETS - 1, head0 + h]
    klane = lax.broadcasted_iota(jnp.int32, (blk, LANES), 1)

    def step(n, bias):
        rows = pl.ds(pl.multiple_of(n * blk, blk), blk)
        kaug = jnp.concatenate([k_ref[0, rows, :], (klane == n).astype(BF16)], axis=1)
        _softmax_update(_dot_nt(qaug_ref[...], kaug) + bias, v_ref[0, rows, :], m, l, acc)

    n_far = jnp.maximum(i - (nd - 1), 0)

    def far_body(n, c):
        step(n, far_bias)
        return c

    def near_body(n, c):
        step(n, bias_ref[0, i - n])
        return c

    lax.fori_loop(0, n_far, far_body, 0)
    lax.fori_loop(n_far, i, near_body, 0)
    o_ref[0] = (acc[...] / l[...]).astype(o_ref.dtype)


def _moba_attention(qkv, bias, rel_bias):
    b, s, _ = qkv.shape
    nd = bias.shape[1]
    blk = MOBA_BLOCK
    nblk = s // blk
    hd = MOBA_HEAD_DIM
    col0 = 3 * DIFF_HEADS
    return pl.pallas_call(
        functools.partial(_moba_kernel, nd=nd, nblk=nblk, head0=DIFF_HEADS),
        grid=(b, MOBA_HEADS, nblk),
        in_specs=[
            pl.BlockSpec(memory_space=pltpu.SMEM),
            pl.BlockSpec((1, blk, hd), lambda bi, h, i: (bi, i, col0 + h)),
            pl.BlockSpec((1, s, hd), lambda bi, h, i: (bi, 0, col0 + MOBA_HEADS + h)),
            pl.BlockSpec((1, s, hd), lambda bi, h, i: (bi, 0, col0 + 2 * MOBA_HEADS + h)),
            pl.BlockSpec((1, nd, blk, blk), lambda bi, h, i: (DIFF_HEADS + h, 0, 0, 0)),
        ],
        out_specs=pl.BlockSpec((1, blk, hd), lambda bi, h, i: (bi, i, h)),
        out_shape=jax.ShapeDtypeStruct((b, s, MOBA_HEADS * hd), BF16),
        scratch_shapes=[pltpu.VMEM((LANES, hd), F32), pltpu.VMEM((blk, 2 * LANES), BF16),
                        pltpu.VMEM((blk, 1), F32), pltpu.VMEM((blk, 1), F32), pltpu.VMEM((blk, hd), F32)],
        compiler_params=_cparams("parallel", "parallel", "arbitrary"),
        name="moba_attention",
    )(rel_bias, qkv, qkv, qkv, bias)


def _mla_prep_kernel(small_ref, qnw_ref, kvnw_ref, wuq_ref, wukv_ref, ct_ref, st_ref,
                     q_ref, k_ref, v_ref, *, scale):
    kw = MLA_HEADS * LANES
    ct = ct_ref[...]
    st = st_ref[...]
    cq = small_ref[:, :MLA_Q_RANK]
    ckv = small_ref[:, MLA_Q_RANK:MLA_Q_RANK + MLA_KV_RANK]
    kpe = small_ref[:, MLA_Q_RANK + MLA_KV_RANK:MLA_Q_RANK + MLA_KV_RANK + LANES]
    kpe_rot = small_ref[:, MLA_Q_RANK + MLA_KV_RANK + LANES:]
    qq = _dot(_rms(cq, qnw_ref[...]).astype(BF16), wuq_ref[...])
    kk = _dot(_rms(ckv, kvnw_ref[...]).astype(BF16), wukv_ref[...])
    k_rope = kpe * ct + kpe_rot * st
    for h in range(MLA_HEADS):
        cols = slice(h * LANES, (h + 1) * LANES)
        rot = qq[:, kw + h * LANES:kw + (h + 1) * LANES]
        q_ref[:, cols] = ((qq[:, cols] * ct + rot * st) * scale).astype(BF16)
        k_ref[:, cols] = (kk[:, cols] + k_rope).astype(BF16)
    v_ref[...] = kk[:, kw:].astype(BF16)


def _mla_prep(small, qnw, kvnw, wuq, wukv, ct, st, tm):
    n = small.shape[0]
    kw = MLA_HEADS * LANES
    vw = MLA_HEADS * MLA_V_DIM
    scale = 1.0 / math.sqrt(MLA_NOPE_DIM + MLA_ROPE_DIM)
    row = lambda i: (i, 0)
    const = lambda i: (0, 0)
    return pl.pallas_call(
        functools.partial(_mla_prep_kernel, scale=scale),
        grid=(n // tm,),
        in_specs=[
            pl.BlockSpec((tm, small.shape[1]), row),
            pl.BlockSpec(qnw.shape, const),
            pl.BlockSpec(kvnw.shape, const),
            pl.BlockSpec(wuq.shape, const),
            pl.BlockSpec(wukv.shape, const),
            pl.BlockSpec((tm, LANES), row),
            pl.BlockSpec((tm, LANES), row),
        ],
        out_specs=[pl.BlockSpec((tm, kw), row), pl.BlockSpec((tm, kw), row), pl.BlockSpec((tm, vw), row)],
        out_shape=[jax.ShapeDtypeStruct((n, kw), BF16), jax.ShapeDtypeStruct((n, kw), BF16),
                   jax.ShapeDtypeStruct((n, vw), BF16)],
        compiler_params=_cparams("parallel"),
        name="mla_prep",
    )(small, qnw, kvnw, wuq, wukv, ct, st)


def _mla_kernel(q_ref, k_ref, v_ref, o_ref, m0, l0, a0, m1, l1, a1, *, tile):
    i = pl.program_id(2)
    stats = ((m0, l0, a0), (m1, l1, a1))
    for st in stats:
        _init_stats(*st)

    def step(j, mask):
        rows = pl.ds(pl.multiple_of(j * tile, tile), tile)
        v = v_ref[0, rows, :]
        for hh, st in enumerate(stats):
            cols = slice(hh * LANES, (hh + 1) * LANES)
            s = _dot_nt(q_ref[0, :, cols], k_ref[0, rows, cols])
            if mask is not None:
                s = s + mask
            _softmax_update(s, v, *st)

    def body(j, c):
        step(j, None)
        return c

    lax.fori_loop(0, i, body, 0)
    r = lax.broadcasted_iota(jnp.int32, (tile, tile), 0)
    c = lax.broadcasted_iota(jnp.int32, (tile, tile), 1)
    step(i, jnp.where(r >= c, 0.0, NEG_INF))
    lane = lax.broadcasted_iota(jnp.int32, (tile, LANES), 1)
    o_ref[0] = jnp.where(lane < MLA_V_DIM, a0[...] / l0[...], a1[...] / l1[...]).astype(o_ref.dtype)


def _mla_attention(q, k, v, tile):
    b, s, _ = q.shape
    pairs = MLA_HEADS // 2
    return pl.pallas_call(
        functools.partial(_mla_kernel, tile=tile),
        grid=(b, pairs, s // tile),
        in_specs=[
            pl.BlockSpec((1, tile, 2 * LANES), lambda bi, h, i: (bi, i, h)),
            pl.BlockSpec((1, s, 2 * LANES), lambda bi, h, i: (bi, 0, h)),
            pl.BlockSpec((1, s, LANES), lambda bi, h, i: (bi, 0, h)),
        ],
        out_specs=pl.BlockSpec((1, tile, LANES), lambda bi, h, i: (bi, i, h)),
        out_shape=jax.ShapeDtypeStruct((b, s, MLA_HEADS * MLA_V_DIM), BF16),
        scratch_shapes=[pltpu.VMEM((tile, 1), F32), pltpu.VMEM((tile, 1), F32), pltpu.VMEM((tile, LANES), F32),
                        pltpu.VMEM((tile, 1), F32), pltpu.VMEM((tile, 1), F32), pltpu.VMEM((tile, LANES), F32)],
        compiler_params=_cparams("parallel", "parallel", "arbitrary"),
        name="mla_attention",
    )(q, k, v)


def _merge_kernel(x_ref, oa_ref, ob_ref, oc_ref, g_ref, wb_ref, wo_ref, nw_ref, o_ref):
    d = x_ref.shape[1]
    mixed = jnp.zeros(x_ref.shape, F32)
    for gi, br_ref in enumerate((oa_ref, ob_ref, oc_ref)):
        br = _dot(br_ref[...], wb_ref[gi])
        gate = 1.0 / (1.0 + jnp.exp(-g_ref[:, gi * d:(gi + 1) * d].astype(F32)))
        mixed = mixed + gate * br
    y = _dot(mixed.astype(BF16), wo_ref[...])
    o_ref[...] = x_ref[...] + _rms(y, nw_ref[...])


def _merge(x, oa, ob, oc, g, wb, wo, nw, tm):
    n, d = x.shape
    row = lambda i: (i, 0)
    return pl.pallas_call(
        _merge_kernel,
        grid=(n // tm,),
        in_specs=[
            pl.BlockSpec((tm, d), row),
            pl.BlockSpec((tm, BRANCH_WIDTH), row),
            pl.BlockSpec((tm, BRANCH_WIDTH), row),
            pl.BlockSpec((tm, BRANCH_WIDTH), row),
            pl.BlockSpec((tm, N_BRANCH * d), row),
            pl.BlockSpec(wb.shape, lambda i: (0, 0, 0)),
            pl.BlockSpec(wo.shape, lambda i: (0, 0)),
            pl.BlockSpec((1, d), lambda i: (0, 0)),
        ],
        out_specs=pl.BlockSpec((tm, d), row),
        out_shape=jax.ShapeDtypeStruct((n, d), F32),
        compiler_params=_cparams("parallel"),
        name="branch_merge",
    )(x, oa, ob, oc, g, wb, wo, nw)


def _mlp_kernel(x_ref, nw1_ref, wu_ref, wd_ref, nw2_ref, o_ref, h_ref, acc_ref):
    j = pl.program_id(1)

    @pl.when(j == 0)
    def _():
        h_ref[...] = _rms(x_ref[...], nw1_ref[...]).astype(BF16)
        acc_ref[...] = jnp.zeros(acc_ref.shape, F32)

    u = jnp.square(jnp.maximum(_dot(h_ref[...], wu_ref[...]), 0.0))
    acc_ref[...] += _dot(u.astype(BF16), wd_ref[...])

    @pl.when(j == pl.num_programs(1) - 1)
    def _():
        o_ref[...] = x_ref[...] + _rms(acc_ref[...], nw2_ref[...])


def _mlp(x, nw1, wu, wd, nw2, tm, tf):
    n, d = x.shape
    f = wu.shape[1]
    return pl.pallas_call(
        _mlp_kernel,
        grid=(n // tm, f // tf),
        in_specs=[
            pl.BlockSpec((tm, d), lambda i, j: (i, 0)),
            pl.BlockSpec((1, d), lambda i, j: (0, 0)),
            pl.BlockSpec((d, tf), lambda i, j: (0, j)),
            pl.BlockSpec((tf, d), lambda i, j: (j, 0)),
            pl.BlockSpec((1, d), lambda i, j: (0, 0)),
        ],
        out_specs=pl.BlockSpec((tm, d), lambda i, j: (i, 0)),
        out_shape=jax.ShapeDtypeStruct((n, d), F32),
        scratch_shapes=[pltpu.VMEM((tm, d), BF16), pltpu.VMEM((tm, d), F32)],
        compiler_params=_cparams("parallel", "arbitrary"),
        name="relu2_mlp",
    )(x, nw1, wu, wd, nw2)


def _rotate_half_cols(w):
    half = w.shape[-1] // 2
    return jnp.concatenate([-w[..., half:], w[..., :half]], axis=-1)


def _pad_cols(w, before, total):
    pad = [(0, 0)] * (w.ndim - 1) + [(before, total - before - w.shape[-1])]
    return jnp.pad(w, pad)


def _layer_weights(w_in, w_uq, w_ukv):
    d = w_in.shape[0]
    n_qkv = 3 * DIFF_HEADS * 2 * DIFF_HEAD_DIM + 3 * MOBA_HEADS * MOBA_HEAD_DIM
    o_cq = n_qkv
    o_ckv = o_cq + MLA_Q_RANK
    o_kpe = o_ckv + MLA_KV_RANK
    o_g = o_kpe + MLA_ROPE_DIM
    w_qkv = w_in[:, :n_qkv].astype(BF16)
    w_kpe = w_in[:, o_kpe:o_g]
    w_small = jnp.concatenate([
        w_in[:, o_cq:o_kpe],
        _pad_cols(w_kpe, MLA_NOPE_DIM, LANES),
        _pad_cols(_rotate_half_cols(w_kpe), MLA_NOPE_DIM, LANES)], axis=1).astype(BF16)
    w_g = w_in[:, o_g:].astype(BF16)

    qk_dim = MLA_NOPE_DIM + MLA_ROPE_DIM
    uq = w_uq.reshape(MLA_Q_RANK, MLA_HEADS, qk_dim)
    uq_plain = _pad_cols(uq, 0, LANES).reshape(MLA_Q_RANK, MLA_HEADS * LANES)
    uq_rot = _pad_cols(_rotate_half_cols(uq[..., MLA_NOPE_DIM:]), MLA_NOPE_DIM, LANES)
    wuq = jnp.concatenate([uq_plain, uq_rot.reshape(MLA_Q_RANK, MLA_HEADS * LANES)], axis=1).astype(BF16)

    ukv = w_ukv.reshape(MLA_KV_RANK, MLA_HEADS, MLA_NOPE_DIM + MLA_V_DIM)
    uk = _pad_cols(ukv[..., :MLA_NOPE_DIM], 0, LANES).reshape(MLA_KV_RANK, MLA_HEADS * LANES)
    uv = ukv[..., MLA_NOPE_DIM:].reshape(MLA_KV_RANK, MLA_HEADS * MLA_V_DIM)
    wukv = jnp.concatenate([uk, uv], axis=1).astype(BF16)
    return w_qkv, w_small, w_g, wuq, wukv


def kernel(x, positions, rel_bias, norm_mix_pre, norm_mix_post, norm_mlp_pre, norm_mlp_post, w_in, diff_lambda, diff_subln, mla_q_norm, mla_w_uq, mla_kv_norm, mla_w_ukv, w_branch, w_out, w_up, w_down):
    b, s, d = x.shape
    n = b * s
    depth = w_in.shape[0]
    tile = ATTN_TILE
    assert s % MOBA_BLOCK == 0 and s % tile == 0 and n % 1024 == 0

    xf = x.reshape(n, d)
    bias = _bias_tiles(rel_bias, tile)

    half = MLA_ROPE_DIM // 2
    inv_freq = ROPE_THETA ** (-jnp.arange(half, dtype=F32) * 2.0 / MLA_ROPE_DIM)
    ang = positions.astype(F32).reshape(n, 1) * inv_freq
    cos, sin = jnp.cos(ang), jnp.sin(ang)
    ct = jnp.concatenate([jnp.ones((n, MLA_NOPE_DIM), F32), cos, cos,
                          jnp.zeros((n, LANES - MLA_NOPE_DIM - MLA_ROPE_DIM), F32)], axis=1)
    st = jnp.concatenate([jnp.zeros((n, MLA_NOPE_DIM), F32), sin, sin,
                          jnp.zeros((n, LANES - MLA_NOPE_DIM - MLA_ROPE_DIM), F32)], axis=1)

    n_diff = DIFF_HEADS * 2 * DIFF_HEAD_DIM
    n_moba = MOBA_HEADS * MOBA_HEAD_DIM
    qkv_scale = jnp.concatenate([
        jnp.full((n_diff,), 1.0 / math.sqrt(DIFF_HEAD_DIM), F32), jnp.ones((2 * n_diff,), F32),
        jnp.full((n_moba,), 1.0 / math.sqrt(MOBA_HEAD_DIM), F32), jnp.ones((2 * n_moba,), F32)]).reshape(1, -1)

    row = lambda a: a.reshape(1, -1)
    for l in range(depth):
        lam_init = 0.8 - 0.6 * math.exp(-0.3 * l)
        w_qkv, w_small, w_g, wuq, wukv = _layer_weights(w_in[l], mla_w_uq[l], mla_w_ukv[l])
        nw = row(norm_mix_pre[l])
        qkv = _norm_matmul(xf, nw, w_qkv, qkv_scale, BF16, 1024, 512)
        small = _norm_matmul(xf, nw, w_small, jnp.ones((1, w_small.shape[1]), F32), F32, 1024, w_small.shape[1])
        g = _norm_matmul(xf, nw, w_g, jnp.ones((1, w_g.shape[1]), F32), F32, 1024, 512)

        qkv3 = qkv.reshape(b, s, -1)
        scal = jnp.full((1,), lam_init, F32)
        oa = _diff_attention(qkv3, bias, rel_bias, scal, diff_lambda[l], row(diff_subln[l]), tile)
        ob = _moba_attention(qkv3, bias, rel_bias)
        mq, mk, mv = _mla_prep(small, row(mla_q_norm[l]), row(mla_kv_norm[l]), wuq, wukv, ct, st, 512)
        oc = _mla_attention(mq.reshape(b, s, -1), mk.reshape(b, s, -1), mv.reshape(b, s, -1), tile)

        xf = _merge(xf, oa.reshape(n, -1), ob.reshape(n, -1), oc.reshape(n, -1), g,
                    w_branch[l].astype(BF16), w_out[l].astype(BF16), row(norm_mix_post[l]), 256)
        xf = _mlp(xf, row(norm_mlp_pre[l]), w_up[l].astype(BF16), w_down[l].astype(BF16),
                  row(norm_mlp_post[l]), 1024, 512)
    return xf.reshape(b, s, d)
```

```python
import functools
import math

import jax
import jax.numpy as jnp
from jax import lax
from jax.experimental import pallas as pl
from jax.experimental.pallas import tpu as pltpu

F32 = jnp.float32
BF16 = jnp.bfloat16

RMS_EPS = 1e-6
NEG_INF = -1e30

T5_BUCKETS = 32
T5_MAX_EXACT = T5_BUCKETS // 2
T5_MAX_DISTANCE = 1024

DIFF_HEADS = 4
DIFF_HEAD_DIM = 64
MOBA_HEADS = 4
MOBA_HEAD_DIM = 128
MOBA_BLOCK = 256
MOBA_TOPK = 3
MLA_HEADS = 8
MLA_Q_RANK = 256
MLA_KV_RANK = 128
MLA_NOPE_DIM = 64
MLA_ROPE_DIM = 32
MLA_V_DIM = 64
ROPE_THETA = 10000.0
N_BRANCH = 3
BRANCH_WIDTH = 512

LOG2E = math.log2(math.e)
SUM_ROWS = 16
LANES = 128
ATTN_TILE = 512
VMEM_LIMIT = 48 * 1024 * 1024


def _cparams(*sem):
    return pltpu.CompilerParams(dimension_semantics=sem, vmem_limit_bytes=VMEM_LIMIT)


def _rms(x, w):
    return x * lax.rsqrt(jnp.mean(x * x, axis=-1, keepdims=True) + RMS_EPS) * w


def _dot(a, b):
    return jnp.dot(a, b, preferred_element_type=F32)


def _dot_nt(a, b):
    return lax.dot_general(a, b, (((1,), (1,)), ((), ())), preferred_element_type=F32)


def _norm_matmul_kernel(x_ref, nw_ref, w_ref, cs_ref, o_ref, h_ref):
    @pl.when(pl.program_id(1) == 0)
    def _():
        h_ref[...] = _rms(x_ref[...], nw_ref[...]).astype(BF16)

    o_ref[...] = (_dot(h_ref[...], w_ref[...]) * cs_ref[...]).astype(o_ref.dtype)


def _norm_matmul(x, nw, w, colscale, out_dtype, tm, tn):
    n, d = x.shape
    nout = w.shape[1]
    return pl.pallas_call(
        _norm_matmul_kernel,
        grid=(n // tm, nout // tn),
        in_specs=[
            pl.BlockSpec((tm, d), lambda i, j: (i, 0)),
            pl.BlockSpec((1, d), lambda i, j: (0, 0)),
            pl.BlockSpec((d, tn), lambda i, j: (0, j)),
            pl.BlockSpec((1, tn), lambda i, j: (0, j)),
        ],
        out_specs=pl.BlockSpec((tm, tn), lambda i, j: (i, j)),
        out_shape=jax.ShapeDtypeStruct((n, nout), out_dtype),
        scratch_shapes=[pltpu.VMEM((tm, d), BF16)],
        compiler_params=_cparams("parallel", "arbitrary"),
        name="norm_matmul",
    )(x, nw, w, colscale)


def _bias_tile_kernel(tab_ref, o_ref, *, tile):
    h = pl.program_id(0)
    d = pl.program_id(1)
    c = lax.broadcasted_iota(jnp.int32, (tile, tile), 0)
    r = lax.broadcasted_iota(jnp.int32, (tile, tile), 1)
    rel = d * tile + r - c
    n = jnp.maximum(rel, 0)
    nf = jnp.maximum(n, 1).astype(F32)
    large = T5_MAX_EXACT + (jnp.log(nf / T5_MAX_EXACT)
                            / math.log(T5_MAX_DISTANCE / T5_MAX_EXACT)
                            * (T5_BUCKETS - T5_MAX_EXACT)).astype(jnp.int32)
    large = jnp.minimum(large, T5_BUCKETS - 1)
    bucket = jnp.where(n < T5_MAX_EXACT, n, large)
    val = jnp.zeros((tile, tile), F32)
    for b in range(T5_BUCKETS):
        val = jnp.where(bucket == b, tab_ref[b, h], val)
    o_ref[0, 0] = jnp.where(rel >= 0, val * LOG2E, NEG_INF)


def _num_bias_tiles(tile):
    return -(-(T5_MAX_DISTANCE + tile - 1) // tile) + 1


def _bias_tiles(rel_bias, tile):
    nh = rel_bias.shape[1]
    nd = _num_bias_tiles(tile)
    return pl.pallas_call(
        functools.partial(_bias_tile_kernel, tile=tile),
        grid=(nh, nd),
        in_specs=[pl.BlockSpec(memory_space=pltpu.SMEM)],
        out_specs=pl.BlockSpec((1, 1, tile, tile), lambda h, d: (h, d, 0, 0)),
        out_shape=jax.ShapeDtypeStruct((nh, nd, tile, tile), F32),
        compiler_params=_cparams("parallel", "parallel"),
        name="t5_bias_tiles",
    )(rel_bias)


def _softmax_update(s, vt, m_ref, acc_ref):
    m_prev = m_ref[...]
    m_new = jnp.maximum(m_prev, jnp.max(s, axis=0, keepdims=True))
    alpha = jnp.exp2(m_prev - m_new)
    p = jnp.exp2(s - m_new)
    acc_ref[...] = alpha * acc_ref[...] + _dot(vt, p.astype(BF16))
    m_ref[...] = m_new


def _pipelined_tiles(n_tiles, fill, consume):
    fill(0, 0)

    def pair(jj, c):
        j = 2 * jj
        fill(j + 1, 1)
        consume(j, 0)
        fill(jnp.minimum(j + 2, n_tiles - 1), 0)
        consume(j + 1, 1)
        return c

    lax.fori_loop(0, lax.shift_right_logical(n_tiles, 1), pair, 0)

    @pl.when(jnp.bitwise_and(n_tiles, 1) == 1)
    def _():
        consume(n_tiles - 1, 0)


def _init_stats(m_ref, acc_ref):
    m_ref[...] = jnp.full(m_ref.shape, NEG_INF, F32)
    acc_ref[...] = jnp.zeros(acc_ref.shape, F32)


def _stat_scratch(tile, dv):
    return [pltpu.VMEM((1, tile), F32), pltpu.VMEM((dv + SUM_ROWS, tile), F32)]


def _normalized(acc_ref, dv):
    return acc_ref[:dv, :] / acc_ref[dv:dv + 1, :]


def _key_rows(j, tile):
    return pl.ds(pl.multiple_of(j * tile, tile), tile)


def _transposed_values(v3, groups, tile):
    b, s, w = v3.shape
    v = v3.reshape(b, s // tile, tile, groups, w // groups)
    v = jnp.concatenate([v, jnp.ones(v.shape[:-1] + (SUM_ROWS,), v.dtype)], axis=-1)
    return v.transpose(0, 3, 1, 4, 2)


def _diff_kernel(scal_ref, lam_ref, subw_ref, q_ref, k_ref, vt_ref, bias_ref, o_ref,
                 qm_ref, s_ref, m1, a1, m2, a2, *, tile, nd):
    i = pl.program_id(2)
    q = q_ref[0]
    lane = lax.broadcasted_iota(jnp.int32, q.shape, 1)
    zero = jnp.zeros_like(q)
    qm_ref[0] = jnp.where(lane < DIFF_HEAD_DIM, q, zero)
    qm_ref[1] = jnp.where(lane >= DIFF_HEAD_DIM, q, zero)
    stats = ((m1, a1), (m2, a2))
    for st in stats:
        _init_stats(*st)

    def fill(j, slot):
        k = k_ref[0, _key_rows(j, tile), :]
        for mp in range(2):
            s_ref[slot, mp] = _dot_nt(k, qm_ref[mp])

    def consume(j, slot):
        bias = bias_ref[0, jnp.minimum(i - j, nd - 1)]
        vt = vt_ref[0, 0, j]
        for mp, st in enumerate(stats):
            _softmax_update(s_ref[slot, mp] + bias, vt, *st)

    _pipelined_tiles(i + 1, fill, consume)

    lam_init = scal_ref[0]
    lv = lam_ref[...]
    lam = (jnp.exp(jnp.sum(lv[0:1] * lv[1:2], axis=1, keepdims=True))
           - jnp.exp(jnp.sum(lv[2:3] * lv[3:4], axis=1, keepdims=True)) + lam_init)
    hd = 2 * DIFF_HEAD_DIM
    o = (_normalized(a1, hd) - lam * _normalized(a2, hd)).T
    o_ref[0] = (_rms(o, subw_ref[...]) * (1.0 - lam_init)).astype(o_ref.dtype)


def _diff_attention(qkv, vt, bias, scal, lam_vecs, subw, tile):
    b, s, _ = qkv.shape
    nd = bias.shape[1]
    hd = 2 * DIFF_HEAD_DIM
    return pl.pallas_call(
        functools.partial(_diff_kernel, tile=tile, nd=nd),
        grid=(b, DIFF_HEADS, s // tile),
        in_specs=[
            pl.BlockSpec(memory_space=pltpu.SMEM),
            pl.BlockSpec(lam_vecs.shape, lambda bi, h, i: (0, 0)),
            pl.BlockSpec((1, hd), lambda bi, h, i: (0, 0)),
            pl.BlockSpec((1, tile, hd), lambda bi, h, i: (bi, i, h)),
            pl.BlockSpec((1, s, hd), lambda bi, h, i: (bi, 0, DIFF_HEADS + h)),
            pl.BlockSpec((1, 1, s // tile, hd + SUM_ROWS, tile), lambda bi, h, i: (bi, h, 0, 0, 0)),
            pl.BlockSpec((1, nd, tile, tile), lambda bi, h, i: (h, 0, 0, 0)),
        ],
        out_specs=pl.BlockSpec((1, tile, hd), lambda bi, h, i: (bi, i, h)),
        out_shape=jax.ShapeDtypeStruct((b, s, DIFF_HEADS * hd), BF16),
        scratch_shapes=[pltpu.VMEM((2, tile, hd), BF16), pltpu.VMEM((2, 2, tile, tile), F32)]
        + _stat_scratch(tile, hd) + _stat_scratch(tile, hd),
        compiler_params=_cparams("parallel", "parallel", "arbitrary"),
        name="diff_attention",
    )(scal, lam_vecs, subw, qkv, qkv, vt, bias)


def _moba_kernel(q_ref, k_ref, oh_ref, vt_ref, bias_ref, o_ref,
                 kmean_ref, qaug_ref, s_ref, m, acc, *, tile, nd, nblk):
    blk = MOBA_BLOCK
    per_tile = tile // blk
    i = pl.program_id(2)

    @pl.when(i == 0)
    def _():
        for n in range(nblk):
            kb = k_ref[0, n * blk:(n + 1) * blk, :].astype(F32)
            kmean_ref[n:n + 1, :] = jnp.mean(kb, axis=0, keepdims=True)

    q = q_ref[0]
    gate = lax.dot_general(kmean_ref[...], q.astype(F32), (((1,), (1,)), ((), ())),
                           precision=lax.Precision.HIGHEST, preferred_element_type=F32)
    blk_id = lax.broadcasted_iota(jnp.int32, gate.shape, 0)
    blk_f = blk_id.astype(F32)
    lane = lax.broadcasted_iota(jnp.int32, gate.shape, 1)
    own = i * per_tile + lax.shift_right_logical(lane, int(math.log2(blk)))
    valid = blk_id < own
    gate = jnp.where(valid, gate, NEG_INF)
    sel = blk_id == own
    for _ in range(MOBA_TOPK):
        top = jnp.max(gate, axis=0, keepdims=True)
        first = jnp.min(jnp.where(gate == top, blk_f, float(nblk)), axis=0, keepdims=True)
        hit = blk_f == first
        sel = jnp.logical_or(sel, jnp.logical_and(hit, valid))
        gate = jnp.where(hit, -3.0e38, gate)
    pen = jnp.where(sel, 0.0, NEG_INF)
    pen = jnp.concatenate([pen, jnp.zeros((LANES - nblk, tile), F32)], axis=0)
    qaug_ref[:, :LANES] = q
    qaug_ref[:, LANES:] = pen.T.astype(BF16)

    _init_stats(m, acc)

    def fill(j, slot):
        rows = _key_rows(j, tile)
        kaug = jnp.concatenate([k_ref[0, rows, :], oh_ref[rows, :]], axis=1)
        s_ref[slot] = _dot_nt(kaug, qaug_ref[...])

    def consume(j, slot):
        bias = bias_ref[0, jnp.minimum(i - j, nd - 1)]
        _softmax_update(s_ref[slot] + bias, vt_ref[0, 0, j], m, acc)

    _pipelined_tiles(i + 1, fill, consume)
    o_ref[0] = _normalized(acc, MOBA_HEAD_DIM).T.astype(o_ref.dtype)


def _moba_attention(qkv, vt, bias, tile):
    b, s, _ = qkv.shape
    nd = bias.shape[1]
    nblk = s // MOBA_BLOCK
    hd = MOBA_HEAD_DIM
    col0 = 3 * DIFF_HEADS
    onehot = (jnp.arange(s)[:, None] // MOBA_BLOCK == jnp.arange(LANES)[None, :]).astype(BF16)
    return pl.pallas_call(
        functools.partial(_moba_kernel, tile=tile, nd=nd, nblk=nblk),
        grid=(b, MOBA_HEADS, s // tile),
        in_specs=[
            pl.BlockSpec((1, tile, hd), lambda bi, h, i: (bi, i, col0 + h)),
            pl.BlockSpec((1, s, hd), lambda bi, h, i: (bi, 0, col0 + MOBA_HEADS + h)),
            pl.BlockSpec((s, LANES), lambda bi, h, i: (0, 0)),
            pl.BlockSpec((1, 1, s // tile, hd + SUM_ROWS, tile), lambda bi, h, i: (bi, h, 0, 0, 0)),
            pl.BlockSpec((1, nd, tile, tile), lambda bi, h, i: (DIFF_HEADS + h, 0, 0, 0)),
        ],
        out_specs=pl.BlockSpec((1, tile, hd), lambda bi, h, i: (bi, i, h)),
        out_shape=jax.ShapeDtypeStruct((b, s, MOBA_HEADS * hd), BF16),
        scratch_shapes=[pltpu.VMEM((nblk, hd), F32), pltpu.VMEM((tile, 2 * LANES), BF16),
                        pltpu.VMEM((2, tile, tile), F32)] + _stat_scratch(tile, hd),
        compiler_params=_cparams("parallel", "parallel", "arbitrary"),
        name="moba_attention",
    )(qkv, qkv, onehot, vt, bias)


def _mla_prep_kernel(small_ref, qnw_ref, kvnw_ref, wuq_ref, wukv_ref, ct_ref, st_ref,
                     q_ref, k_ref, v_ref, *, scale):
    kw = MLA_HEADS * LANES
    ct = ct_ref[...]
    st = st_ref[...]
    cq = small_ref[:, :MLA_Q_RANK]
    ckv = small_ref[:, MLA_Q_RANK:MLA_Q_RANK + MLA_KV_RANK]
    kpe = small_ref[:, MLA_Q_RANK + MLA_KV_RANK:MLA_Q_RANK + MLA_KV_RANK + LANES]
    kpe_rot = small_ref[:, MLA_Q_RANK + MLA_KV_RANK + LANES:]
    qq = _dot(_rms(cq, qnw_ref[...]).astype(BF16), wuq_ref[...])
    kk = _dot(_rms(ckv, kvnw_ref[...]).astype(BF16), wukv_ref[...])
    k_rope = kpe * ct + kpe_rot * st
    for h in range(MLA_HEADS):
        cols = slice(h * LANES, (h + 1) * LANES)
        rot = qq[:, kw + h * LANES:kw + (h + 1) * LANES]
        q_ref[:, cols] = ((qq[:, cols] * ct + rot * st) * scale).astype(BF16)
        k_ref[:, cols] = (kk[:, cols] + k_rope).astype(BF16)
    v_ref[...] = kk[:, kw:].astype(BF16)


def _mla_prep(small, qnw, kvnw, wuq, wukv, ct, st, tm):
    n = small.shape[0]
    kw = MLA_HEADS * LANES
    vw = MLA_HEADS * MLA_V_DIM
    scale = LOG2E / math.sqrt(MLA_NOPE_DIM + MLA_ROPE_DIM)
    row = lambda i: (i, 0)
    const = lambda i: (0, 0)
    return pl.pallas_call(
        functools.partial(_mla_prep_kernel, scale=scale),
        grid=(n // tm,),
        in_specs=[
            pl.BlockSpec((tm, small.shape[1]), row),
            pl.BlockSpec(qnw.shape, const),
            pl.BlockSpec(kvnw.shape, const),
            pl.BlockSpec(wuq.shape, const),
            pl.BlockSpec(wukv.shape, const),
            pl.BlockSpec((tm, LANES), row),
            pl.BlockSpec((tm, LANES), row),
        ],
        out_specs=[pl.BlockSpec((tm, kw), row), pl.BlockSpec((tm, kw), row), pl.BlockSpec((tm, vw), row)],
        out_shape=[jax.ShapeDtypeStruct((n, kw), BF16), jax.ShapeDtypeStruct((n, kw), BF16),
                   jax.ShapeDtypeStruct((n, vw), BF16)],
        compiler_params=_cparams("parallel"),
        name="mla_prep",
    )(small, qnw, kvnw, wuq, wukv, ct, st)


def _mla_kernel(q_ref, k_ref, vt_ref, o_ref, mask_ref, s_ref, m0, a0, m1, a1, *, tile):
    i = pl.program_id(2)
    stats = ((m0, a0), (m1, a1))
    for st in stats:
        _init_stats(*st)
    key = lax.broadcasted_iota(jnp.int32, (tile, tile), 0)
    qry = lax.broadcasted_iota(jnp.int32, (tile, tile), 1)
    mask_ref[0] = jnp.where(qry >= key, 0.0, NEG_INF)
    mask_ref[1] = jnp.zeros((tile, tile), F32)

    def fill(j, slot):
        rows = _key_rows(j, tile)
        for hh in range(2):
            cols = slice(hh * LANES, (hh + 1) * LANES)
            s_ref[slot, hh] = _dot_nt(k_ref[0, rows, cols], q_ref[0, :, cols])

    def consume(j, slot):
        mask = mask_ref[jnp.minimum(i - j, 1)]
        for hh, st in enumerate(stats):
            _softmax_update(s_ref[slot, hh] + mask, vt_ref[0, hh, j], *st)

    _pipelined_tiles(i + 1, fill, consume)
    o = jnp.concatenate([_normalized(a0, MLA_V_DIM), _normalized(a1, MLA_V_DIM)], axis=0)
    o_ref[0] = o.T.astype(o_ref.dtype)


def _mla_attention(q, k, vt, tile):
    b, s, _ = q.shape
    pairs = MLA_HEADS // 2
    return pl.pallas_call(
        functools.partial(_mla_kernel, tile=tile),
        grid=(b, pairs, s // tile),
        in_specs=[
            pl.BlockSpec((1, tile, 2 * LANES), lambda bi, h, i: (bi, i, h)),
            pl.BlockSpec((1, s, 2 * LANES), lambda bi, h, i: (bi, 0, h)),
            pl.BlockSpec((1, 2, s // tile, MLA_V_DIM + SUM_ROWS, tile), lambda bi, h, i: (bi, h, 0, 0, 0)),
        ],
        out_specs=pl.BlockSpec((1, tile, 2 * MLA_V_DIM), lambda bi, h, i: (bi, i, h)),
        out_shape=jax.ShapeDtypeStruct((b, s, MLA_HEADS * MLA_V_DIM), BF16),
        scratch_shapes=[pltpu.VMEM((2, tile, tile), F32), pltpu.VMEM((2, 2, tile, tile), F32)]
        + _stat_scratch(tile, MLA_V_DIM) + _stat_scratch(tile, MLA_V_DIM),
        compiler_params=_cparams("parallel", "parallel", "arbitrary"),
        name="mla_attention",
    )(q, k, vt)


def _merge_kernel(x_ref, oa_ref, ob_ref, oc_ref, g_ref, wb_ref, wo_ref, nw_ref, o_ref):
    d = x_ref.shape[1]
    mixed = jnp.zeros(x_ref.shape, F32)
    for gi, br_ref in enumerate((oa_ref, ob_ref, oc_ref)):
        br = _dot(br_ref[...], wb_ref[gi])
        gate = 1.0 / (1.0 + jnp.exp(-g_ref[:, gi * d:(gi + 1) * d].astype(F32)))
        mixed = mixed + gate * br
    y = _dot(mixed.astype(BF16), wo_ref[...])
    o_ref[...] = x_ref[...] + _rms(y, nw_ref[...])


def _merge(x, oa, ob, oc, g, wb, wo, nw, tm):
    n, d = x.shape
    row = lambda i: (i, 0)
    return pl.pallas_call(
        _merge_kernel,
        grid=(n // tm,),
        in_specs=[
            pl.BlockSpec((tm, d), row),
            pl.BlockSpec((tm, BRANCH_WIDTH), row),
            pl.BlockSpec((tm, BRANCH_WIDTH), row),
            pl.BlockSpec((tm, BRANCH_WIDTH), row),
            pl.BlockSpec((tm, N_BRANCH * d), row),
            pl.BlockSpec(wb.shape, lambda i: (0, 0, 0)),
            pl.BlockSpec(wo.shape, lambda i: (0, 0)),
            pl.BlockSpec((1, d), lambda i: (0, 0)),
        ],
        out_specs=pl.BlockSpec((tm, d), row),
        out_shape=jax.ShapeDtypeStruct((n, d), F32),
        compiler_params=_cparams("parallel"),
        name="branch_merge",
    )(x, oa, ob, oc, g, wb, wo, nw)


def _mlp_kernel(x_ref, nw1_ref, wu_ref, wd_ref, nw2_ref, o_ref, h_ref, acc_ref):
    j = pl.program_id(1)

    @pl.when(j == 0)
    def _():
        h_ref[...] = _rms(x_ref[...], nw1_ref[...]).astype(BF16)
        acc_ref[...] = jnp.zeros(acc_ref.shape, F32)

    u = jnp.square(jnp.maximum(_dot(h_ref[...], wu_ref[...]), 0.0))
    acc_ref[...] += _dot(u.astype(BF16), wd_ref[...])

    @pl.when(j == pl.num_programs(1) - 1)
    def _():
        o_ref[...] = x_ref[...] + _rms(acc_ref[...], nw2_ref[...])


def _mlp(x, nw1, wu, wd, nw2, tm, tf):
    n, d = x.shape
    f = wu.shape[1]
    return pl.pallas_call(
        _mlp_kernel,
        grid=(n // tm, f // tf),
        in_specs=[
            pl.BlockSpec((tm, d), lambda i, j: (i, 0)),
            pl.BlockSpec((1, d), lambda i, j: (0, 0)),
            pl.BlockSpec((d, tf), lambda i, j: (0, j)),
            pl.BlockSpec((tf, d), lambda i, j: (j, 0)),
            pl.BlockSpec((1, d), lambda i, j: (0, 0)),
        ],
        out_specs=pl.BlockSpec((tm, d), lambda i, j: (i, 0)),
        out_shape=jax.ShapeDtypeStruct((n, d), F32),
        scratch_shapes=[pltpu.VMEM((tm, d), BF16), pltpu.VMEM((tm, d), F32)],
        compiler_params=_cparams("parallel", "arbitrary"),
        name="relu2_mlp",
    )(x, nw1, wu, wd, nw2)


def _rotate_half_cols(w):
    half = w.shape[-1] // 2
    return jnp.concatenate([-w[..., half:], w[..., :half]], axis=-1)


def _pad_cols(w, before, total):
    pad = [(0, 0)] * (w.ndim - 1) + [(before, total - before - w.shape[-1])]
    return jnp.pad(w, pad)


def _layer_weights(w_in, w_uq, w_ukv):
    n_qkv = 3 * DIFF_HEADS * 2 * DIFF_HEAD_DIM + 3 * MOBA_HEADS * MOBA_HEAD_DIM
    o_cq = n_qkv
    o_ckv = o_cq + MLA_Q_RANK
    o_kpe = o_ckv + MLA_KV_RANK
    o_g = o_kpe + MLA_ROPE_DIM
    w_qkv = w_in[:, :n_qkv].astype(BF16)
    w_kpe = w_in[:, o_kpe:o_g]
    w_small = jnp.concatenate([
        w_in[:, o_cq:o_kpe],
        _pad_cols(w_kpe, MLA_NOPE_DIM, LANES),
        _pad_cols(_rotate_half_cols(w_kpe), MLA_NOPE_DIM, LANES)], axis=1).astype(BF16)
    w_g = w_in[:, o_g:].astype(BF16)

    qk_dim = MLA_NOPE_DIM + MLA_ROPE_DIM
    uq = w_uq.reshape(MLA_Q_RANK, MLA_HEADS, qk_dim)
    uq_plain = _pad_cols(uq, 0, LANES).reshape(MLA_Q_RANK, MLA_HEADS * LANES)
    uq_rot = _pad_cols(_rotate_half_cols(uq[..., MLA_NOPE_DIM:]), MLA_NOPE_DIM, LANES)
    wuq = jnp.concatenate([uq_plain, uq_rot.reshape(MLA_Q_RANK, MLA_HEADS * LANES)], axis=1).astype(BF16)

    ukv = w_ukv.reshape(MLA_KV_RANK, MLA_HEADS, MLA_NOPE_DIM + MLA_V_DIM)
    uk = _pad_cols(ukv[..., :MLA_NOPE_DIM], 0, LANES).reshape(MLA_KV_RANK, MLA_HEADS * LANES)
    uv = ukv[..., MLA_NOPE_DIM:].reshape(MLA_KV_RANK, MLA_HEADS * MLA_V_DIM)
    wukv = jnp.concatenate([uk, uv], axis=1).astype(BF16)
    return w_qkv, w_small, w_g, wuq, wukv


def kernel(x, positions, rel_bias, norm_mix_pre, norm_mix_post, norm_mlp_pre, norm_mlp_post, w_in, diff_lambda, diff_subln, mla_q_norm, mla_w_uq, mla_kv_norm, mla_w_ukv, w_branch, w_out, w_up, w_down):
    b, s, d = x.shape
    n = b * s
    depth = w_in.shape[0]
    tile = ATTN_TILE
    assert tile % MOBA_BLOCK == 0 and s % tile == 0 and n % 1024 == 0 and (s // MOBA_BLOCK) % 8 == 0

    xf = x.reshape(n, d)
    bias = _bias_tiles(rel_bias, tile)

    half = MLA_ROPE_DIM // 2
    inv_freq = ROPE_THETA ** (-jnp.arange(half, dtype=F32) * 2.0 / MLA_ROPE_DIM)
    ang = positions.astype(F32).reshape(n, 1) * inv_freq
    cos, sin = jnp.cos(ang), jnp.sin(ang)
    ct = jnp.concatenate([jnp.ones((n, MLA_NOPE_DIM), F32), cos, cos,
                          jnp.zeros((n, LANES - MLA_NOPE_DIM - MLA_ROPE_DIM), F32)], axis=1)
    st = jnp.concatenate([jnp.zeros((n, MLA_NOPE_DIM), F32), sin, sin,
                          jnp.zeros((n, LANES - MLA_NOPE_DIM - MLA_ROPE_DIM), F32)], axis=1)

    n_diff = DIFF_HEADS * 2 * DIFF_HEAD_DIM
    n_moba = MOBA_HEADS * MOBA_HEAD_DIM
    qkv_scale = jnp.concatenate([
        jnp.full((n_diff,), LOG2E / math.sqrt(DIFF_HEAD_DIM), F32), jnp.ones((2 * n_diff,), F32),
        jnp.full((n_moba,), LOG2E / math.sqrt(MOBA_HEAD_DIM), F32), jnp.ones((2 * n_moba,), F32)]).reshape(1, -1)

    row = lambda a: a.reshape(1, -1)
    for l in range(depth):
        lam_init = 0.8 - 0.6 * math.exp(-0.3 * l)
        w_qkv, w_small, w_g, wuq, wukv = _layer_weights(w_in[l], mla_w_uq[l], mla_w_ukv[l])
        nw = row(norm_mix_pre[l])
        qkv = _norm_matmul(xf, nw, w_qkv, qkv_scale, BF16, 1024, 512)
        small = _norm_matmul(xf, nw, w_small, jnp.ones((1, w_small.shape[1]), F32), F32, 1024, w_small.shape[1])
        g = _norm_matmul(xf, nw, w_g, jnp.ones((1, w_g.shape[1]), F32), F32, 1024, 512)

        qkv3 = qkv.reshape(b, s, -1)
        scal = jnp.full((1,), lam_init, F32)
        diff_vt = _transposed_values(qkv3[:, :, 2 * n_diff:3 * n_diff], DIFF_HEADS, tile)
        oa = _diff_attention(qkv3, diff_vt, bias, scal, diff_lambda[l], row(diff_subln[l]), tile)
        moba_vt = _transposed_values(qkv3[:, :, 3 * n_diff + 2 * n_moba:], MOBA_HEADS, tile)
        ob = _moba_attention(qkv3, moba_vt, bias, tile)
        mq, mk, mv = _mla_prep(small, row(mla_q_norm[l]), row(mla_kv_norm[l]), wuq, wukv, ct, st, 512)
        mla_vt = _transposed_values(mv.reshape(b, s, -1), MLA_HEADS, tile)
        oc = _mla_attention(mq.reshape(b, s, -1), mk.reshape(b, s, -1), mla_vt, tile)

        xf = _merge(xf, oa.reshape(n, -1), ob.reshape(n, -1), oc.reshape(n, -1), g,
                    w_branch[l].astype(BF16), w_out[l].astype(BF16), row(norm_mix_post[l]), 256)
        xf = _mlp(xf, row(norm_mlp_pre[l]), w_up[l].astype(BF16), w_down[l].astype(BF16),
                  row(norm_mlp_post[l]), 1024, 512)
    return xf.reshape(b, s, d)
```

```python
import functools
import math

import jax
import jax.numpy as jnp
from jax import lax
from jax.experimental import pallas as pl
from jax.experimental.pallas import tpu as pltpu

F32 = jnp.float32
BF16 = jnp.bfloat16

RMS_EPS = 1e-6
NEG_INF = -1e30
STAT_INIT = -(2.0 ** 100)

T5_BUCKETS = 32
T5_MAX_EXACT = T5_BUCKETS // 2
T5_MAX_DISTANCE = 1024

DIFF_HEADS = 4
DIFF_HEAD_DIM = 64
MOBA_HEADS = 4
MOBA_HEAD_DIM = 128
MOBA_BLOCK = 256
MOBA_TOPK = 3
MLA_HEADS = 8
MLA_Q_RANK = 256
MLA_KV_RANK = 128
MLA_NOPE_DIM = 64
MLA_ROPE_DIM = 32
MLA_V_DIM = 64
ROPE_THETA = 10000.0
N_BRANCH = 3
BRANCH_WIDTH = 512

LOG2E = math.log2(math.e)
MOBA_HEADS_PER_STEP = 2
MLA_HEADS_PER_STEP = 4
QUERY_SPLIT = 1
SUM_ROWS = 16
LANES = 128
ATTN_TILE = 512
VMEM_LIMIT = 48 * 1024 * 1024


def _cparams(*sem):
    return pltpu.CompilerParams(dimension_semantics=sem, vmem_limit_bytes=VMEM_LIMIT)


def _rms(x, w):
    return x * lax.rsqrt(jnp.mean(x * x, axis=-1, keepdims=True) + RMS_EPS) * w


def _dot(a, b):
    return jnp.dot(a, b, preferred_element_type=F32)


def _dot_nt(a, b):
    return lax.dot_general(a, b, (((1,), (1,)), ((), ())), preferred_element_type=F32)


def _norm_matmul_kernel(x_ref, nw_ref, w_ref, cs_ref, o_ref, h_ref):
    @pl.when(pl.program_id(1) == 0)
    def _():
        h_ref[...] = _rms(x_ref[...], nw_ref[...]).astype(BF16)

    o_ref[...] = (_dot(h_ref[...], w_ref[...]) * cs_ref[...]).astype(o_ref.dtype)


def _norm_matmul(x, nw, w, colscale, out_dtype, tm, tn):
    n, d = x.shape
    nout = w.shape[1]
    return pl.pallas_call(
        _norm_matmul_kernel,
        grid=(n // tm, nout // tn),
        in_specs=[
            pl.BlockSpec((tm, d), lambda i, j: (i, 0)),
            pl.BlockSpec((1, d), lambda i, j: (0, 0)),
            pl.BlockSpec((d, tn), lambda i, j: (0, j)),
            pl.BlockSpec((1, tn), lambda i, j: (0, j)),
        ],
        out_specs=pl.BlockSpec((tm, tn), lambda i, j: (i, j)),
        out_shape=jax.ShapeDtypeStruct((n, nout), out_dtype),
        scratch_shapes=[pltpu.VMEM((tm, d), BF16)],
        compiler_params=_cparams("parallel", "arbitrary"),
        name="norm_matmul",
    )(x, nw, w, colscale)


def _bias_tile_kernel(tab_ref, o_ref, *, tile):
    h = pl.program_id(0)
    d = pl.program_id(1)
    c = lax.broadcasted_iota(jnp.int32, (tile, tile), 0)
    r = lax.broadcasted_iota(jnp.int32, (tile, tile), 1)
    rel = d * tile + r - c
    n = jnp.maximum(rel, 0)
    nf = jnp.maximum(n, 1).astype(F32)
    large = T5_MAX_EXACT + (jnp.log(nf / T5_MAX_EXACT)
                            / math.log(T5_MAX_DISTANCE / T5_MAX_EXACT)
                            * (T5_BUCKETS - T5_MAX_EXACT)).astype(jnp.int32)
    large = jnp.minimum(large, T5_BUCKETS - 1)
    bucket = jnp.where(n < T5_MAX_EXACT, n, large)
    val = jnp.zeros((tile, tile), F32)
    for b in range(T5_BUCKETS):
        val = jnp.where(bucket == b, tab_ref[b, h], val)
    o_ref[0, 0] = jnp.where(rel >= 0, val * LOG2E, NEG_INF)


def _num_bias_tiles(tile):
    return -(-(T5_MAX_DISTANCE + tile - 1) // tile) + 1


def _bias_tiles(rel_bias, tile):
    nh = rel_bias.shape[1]
    nd = _num_bias_tiles(tile)
    return pl.pallas_call(
        functools.partial(_bias_tile_kernel, tile=tile),
        grid=(nh, nd),
        in_specs=[pl.BlockSpec(memory_space=pltpu.SMEM)],
        out_specs=pl.BlockSpec((1, 1, tile, tile), lambda h, d: (h, d, 0, 0)),
        out_shape=jax.ShapeDtypeStruct((nh, nd, tile, tile), F32),
        compiler_params=_cparams("parallel", "parallel"),
        name="t5_bias_tiles",
    )(rel_bias)


def _store_scores(s, s_ref, mx_ref, cols):
    s_ref[:, cols] = s.astype(BF16)
    mx_ref[:, cols] = jnp.max(s, axis=0, keepdims=True).astype(BF16).astype(F32)


def _softmax_update(s_ref, mx_ref, vt, m_ref, acc_ref, cols):
    m_prev = m_ref[:, cols]
    m_new = jnp.maximum(m_prev, mx_ref[:, cols])
    alpha = jnp.exp2(m_prev - m_new)
    p = jnp.exp2(s_ref[:, cols] - m_new.astype(BF16))
    acc_ref[:, cols] = alpha * acc_ref[:, cols] + _dot(vt, p)
    m_ref[:, cols] = m_new


def _pipelined_tiles(n_tiles, n_maps, tile, fill, consume):
    half = tile // QUERY_SPLIT
    pieces = [(mp, slice(h * half, (h + 1) * half)) for mp in range(n_maps) for h in range(QUERY_SPLIT)]
    for mp, cols in pieces:
        fill(0, 0, mp, cols)

    def step(j_fill, slot_fill, j_use, slot_use):
        for mp, cols in pieces:
            fill(j_fill, slot_fill, mp, cols)
            consume(j_use, slot_use, mp, cols)

    def pair(jj, c):
        j = 2 * jj
        step(j + 1, 1, j, 0)
        step(jnp.minimum(j + 2, n_tiles - 1), 0, j + 1, 1)
        return c

    lax.fori_loop(0, lax.shift_right_logical(n_tiles, 1), pair, 0)

    @pl.when(jnp.bitwise_and(n_tiles, 1) == 1)
    def _():
        for mp, cols in pieces:
            consume(n_tiles - 1, 0, mp, cols)


def _init_stats(m_ref, acc_ref):
    m_ref[...] = jnp.full(m_ref.shape, STAT_INIT, F32)
    acc_ref[...] = jnp.zeros(acc_ref.shape, F32)


def _stat_scratch(tile, dv):
    return [pltpu.VMEM((1, tile), F32), pltpu.VMEM((dv + SUM_ROWS, tile), F32)]


def _score_scratch(n_maps, tile):
    return [pltpu.VMEM((2, n_maps, tile, tile), BF16), pltpu.VMEM((2, n_maps, 1, tile), F32)]


def _normalized(acc_ref, dv):
    return acc_ref[:dv, :] / acc_ref[dv:dv + 1, :]


def _key_rows(j, tile):
    return pl.ds(pl.multiple_of(j * tile, tile), tile)


def _transposed_values(v3, groups, tile):
    b, s, w = v3.shape
    v = v3.reshape(b, s // tile, tile, groups, w // groups)
    v = jnp.concatenate([v, jnp.ones(v.shape[:-1] + (SUM_ROWS,), v.dtype)], axis=-1)
    return v.transpose(0, 3, 1, 4, 2)


def _diff_kernel(scal_ref, lam_ref, subw_ref, q_ref, k_ref, vt_ref, bias_ref, o_ref,
                 qm_ref, s_ref, mx_ref, m1, a1, m2, a2, *, tile, nd):
    i = pl.program_id(2)
    q = q_ref[0]
    lane = lax.broadcasted_iota(jnp.int32, q.shape, 1)
    zero = jnp.zeros_like(q)
    qm_ref[0] = jnp.where(lane < DIFF_HEAD_DIM, q, zero)
    qm_ref[1] = jnp.where(lane >= DIFF_HEAD_DIM, q, zero)
    stats = ((m1, a1), (m2, a2))
    for st in stats:
        _init_stats(*st)

    def fill(j, slot, mp, cols):
        k = k_ref[0, _key_rows(j, tile), :]
        bias = bias_ref[0, jnp.minimum(i - j, nd - 1), :, cols]
        _store_scores(_dot_nt(k, qm_ref[mp, cols, :]) + bias, s_ref.at[slot, mp], mx_ref.at[slot, mp], cols)

    def consume(j, slot, mp, cols):
        _softmax_update(s_ref.at[slot, mp], mx_ref.at[slot, mp], vt_ref[0, 0, j], *stats[mp], cols)

    _pipelined_tiles(i + 1, 2, tile, fill, consume)

    lam_init = scal_ref[0]
    lv = lam_ref[...]
    lam = (jnp.exp(jnp.sum(lv[0:1] * lv[1:2], axis=1, keepdims=True))
           - jnp.exp(jnp.sum(lv[2:3] * lv[3:4], axis=1, keepdims=True)) + lam_init)
    hd = 2 * DIFF_HEAD_DIM
    o = (_normalized(a1, hd) - lam * _normalized(a2, hd)).T
    o_ref[0] = (_rms(o, subw_ref[...]) * (1.0 - lam_init)).astype(o_ref.dtype)


def _diff_attention(qkv, vt, bias, scal, lam_vecs, subw, tile):
    b, s, _ = qkv.shape
    nd = bias.shape[1]
    hd = 2 * DIFF_HEAD_DIM
    return pl.pallas_call(
        functools.partial(_diff_kernel, tile=tile, nd=nd),
        grid=(b, DIFF_HEADS, s // tile),
        in_specs=[
            pl.BlockSpec(memory_space=pltpu.SMEM),
            pl.BlockSpec(lam_vecs.shape, lambda bi, h, i: (0, 0)),
            pl.BlockSpec((1, hd), lambda bi, h, i: (0, 0)),
            pl.BlockSpec((1, tile, hd), lambda bi, h, i: (bi, i, h)),
            pl.BlockSpec((1, s, hd), lambda bi, h, i: (bi, 0, DIFF_HEADS + h)),
            pl.BlockSpec((1, 1, s // tile, hd + SUM_ROWS, tile), lambda bi, h, i: (bi, h, 0, 0, 0)),
            pl.BlockSpec((1, nd, tile, tile), lambda bi, h, i: (h, 0, 0, 0)),
        ],
        out_specs=pl.BlockSpec((1, tile, hd), lambda bi, h, i: (bi, i, h)),
        out_shape=jax.ShapeDtypeStruct((b, s, DIFF_HEADS * hd), BF16),
        scratch_shapes=[pltpu.VMEM((2, tile, hd), BF16)] + _score_scratch(2, tile)
        + _stat_scratch(tile, hd) + _stat_scratch(tile, hd),
        compiler_params=_cparams("parallel", "parallel", "arbitrary"),
        name="diff_attention",
    )(scal, lam_vecs, subw, qkv, qkv, vt, bias)


def _moba_kernel(q_ref, k_ref, oh_ref, vt_ref, bias_ref, o_ref,
                 kmean_ref, qaug_ref, s_ref, mx_ref, *stat_refs, tile, nd, nblk, heads):
    blk = MOBA_BLOCK
    hd = MOBA_HEAD_DIM
    per_tile = tile // blk
    i = pl.program_id(2)
    stats = [stat_refs[2 * hh:2 * hh + 2] for hh in range(heads)]

    @pl.when(i == 0)
    def _():
        for hh in range(heads):
            for n in range(nblk):
                kb = k_ref[0, n * blk:(n + 1) * blk, hh * hd:(hh + 1) * hd].astype(F32)
                mean = jnp.mean(kb, axis=0, keepdims=True)
                for part in range(3):
                    term = mean.astype(BF16).astype(F32)
                    kmean_ref[hh, part * nblk + n:part * nblk + n + 1, :] = term
                    mean = mean - term

    for hh in range(heads):
        q = q_ref[0, :, hh * hd:(hh + 1) * hd]
        parts = _dot_nt(kmean_ref[hh].astype(BF16), q)
        gate = parts[:nblk] + parts[nblk:2 * nblk] + parts[2 * nblk:]
        blk_id = lax.broadcasted_iota(jnp.int32, gate.shape, 0)
        blk_f = blk_id.astype(F32)
        lane = lax.broadcasted_iota(jnp.int32, gate.shape, 1)
        own = i * per_tile + lax.shift_right_logical(lane, int(math.log2(blk)))
        valid = blk_id < own
        gate = jnp.where(valid, gate, NEG_INF)
        sel = blk_id == own
        for _ in range(MOBA_TOPK):
            top = jnp.max(gate, axis=0, keepdims=True)
            first = jnp.min(jnp.where(gate == top, blk_f, float(nblk)), axis=0, keepdims=True)
            hit = blk_f == first
            sel = jnp.logical_or(sel, jnp.logical_and(hit, valid))
            gate = jnp.where(hit, -3.0e38, gate)
        pen = jnp.where(sel, 0.0, NEG_INF)
        pen = jnp.concatenate([pen, jnp.zeros((LANES - nblk, tile), F32)], axis=0)
        qaug_ref[hh, :, :LANES] = q
        qaug_ref[hh, :, LANES:] = pen.T.astype(BF16)
        _init_stats(*stats[hh])

    def fill(j, slot, hh, cols):
        rows = _key_rows(j, tile)
        kaug = jnp.concatenate([k_ref[0, rows, hh * hd:(hh + 1) * hd], oh_ref[rows, :]], axis=1)
        bias = bias_ref[hh, jnp.minimum(i - j, nd - 1), :, cols]
        _store_scores(_dot_nt(kaug, qaug_ref[hh, cols, :]) + bias, s_ref.at[slot, hh], mx_ref.at[slot, hh], cols)

    def consume(j, slot, hh, cols):
        _softmax_update(s_ref.at[slot, hh], mx_ref.at[slot, hh], vt_ref[0, hh, j], *stats[hh], cols)

    _pipelined_tiles(i + 1, heads, tile, fill, consume)
    for hh in range(heads):
        o_ref[0, :, hh * hd:(hh + 1) * hd] = _normalized(stats[hh][1], hd).T.astype(o_ref.dtype)


def _moba_attention(qkv, vt, bias, tile):
    b, s, _ = qkv.shape
    nd = bias.shape[1]
    nblk = s // MOBA_BLOCK
    hd = MOBA_HEAD_DIM
    hg = MOBA_HEADS_PER_STEP
    w = hg * hd
    col0 = 3 * DIFF_HEADS // hg
    groups = MOBA_HEADS // hg
    onehot = (jnp.arange(s)[:, None] // MOBA_BLOCK == jnp.arange(LANES)[None, :]).astype(BF16)
    return pl.pallas_call(
        functools.partial(_moba_kernel, tile=tile, nd=nd, nblk=nblk, heads=hg),
        grid=(b, groups, s // tile),
        in_specs=[
            pl.BlockSpec((1, tile, w), lambda bi, h, i: (bi, i, col0 + h)),
            pl.BlockSpec((1, s, w), lambda bi, h, i: (bi, 0, col0 + groups + h)),
            pl.BlockSpec((s, LANES), lambda bi, h, i: (0, 0)),
            pl.BlockSpec((1, hg, s // tile, hd + SUM_ROWS, tile), lambda bi, h, i: (bi, h, 0, 0, 0)),
            pl.BlockSpec((hg, nd, tile, tile), lambda bi, h, i: (DIFF_HEADS // hg + h, 0, 0, 0)),
        ],
        out_specs=pl.BlockSpec((1, tile, w), lambda bi, h, i: (bi, i, h)),
        out_shape=jax.ShapeDtypeStruct((b, s, MOBA_HEADS * hd), BF16),
        scratch_shapes=[pltpu.VMEM((hg, 3 * nblk, hd), F32), pltpu.VMEM((hg, tile, 2 * LANES), BF16)]
        + _score_scratch(hg, tile) + hg * _stat_scratch(tile, hd),
        compiler_params=_cparams("parallel", "parallel", "arbitrary"),
        name="moba_attention",
    )(qkv, qkv, onehot, vt, bias)


def _mla_prep_kernel(small_ref, qnw_ref, kvnw_ref, wuq_ref, wukv_ref, ct_ref, st_ref,
                     q_ref, k_ref, v_ref, *, scale):
    kw = MLA_HEADS * LANES
    ct = ct_ref[...]
    st = st_ref[...]
    cq = small_ref[:, :MLA_Q_RANK]
    ckv = small_ref[:, MLA_Q_RANK:MLA_Q_RANK + MLA_KV_RANK]
    kpe = small_ref[:, MLA_Q_RANK + MLA_KV_RANK:MLA_Q_RANK + MLA_KV_RANK + LANES]
    kpe_rot = small_ref[:, MLA_Q_RANK + MLA_KV_RANK + LANES:]
    qq = _dot(_rms(cq, qnw_ref[...]).astype(BF16), wuq_ref[...])
    kk = _dot(_rms(ckv, kvnw_ref[...]).astype(BF16), wukv_ref[...])
    k_rope = kpe * ct + kpe_rot * st
    for h in range(MLA_HEADS):
        cols = slice(h * LANES, (h + 1) * LANES)
        rot = qq[:, kw + h * LANES:kw + (h + 1) * LANES]
        q_ref[:, cols] = ((qq[:, cols] * ct + rot * st) * scale).astype(BF16)
        k_ref[:, cols] = (kk[:, cols] + k_rope).astype(BF16)
    v_ref[...] = kk[:, kw:].astype(BF16)


def _mla_prep(small, qnw, kvnw, wuq, wukv, ct, st, tm):
    n = small.shape[0]
    kw = MLA_HEADS * LANES
    vw = MLA_HEADS * MLA_V_DIM
    scale = LOG2E / math.sqrt(MLA_NOPE_DIM + MLA_ROPE_DIM)
    row = lambda i: (i, 0)
    const = lambda i: (0, 0)
    return pl.pallas_call(
        functools.partial(_mla_prep_kernel, scale=scale),
        grid=(n // tm,),
        in_specs=[
            pl.BlockSpec((tm, small.shape[1]), row),
            pl.BlockSpec(qnw.shape, const),
            pl.BlockSpec(kvnw.shape, const),
            pl.BlockSpec(wuq.shape, const),
            pl.BlockSpec(wukv.shape, const),
            pl.BlockSpec((tm, LANES), row),
            pl.BlockSpec((tm, LANES), row),
        ],
        out_specs=[pl.BlockSpec((tm, kw), row), pl.BlockSpec((tm, kw), row), pl.BlockSpec((tm, vw), row)],
        out_shape=[jax.ShapeDtypeStruct((n, kw), BF16), jax.ShapeDtypeStruct((n, kw), BF16),
                   jax.ShapeDtypeStruct((n, vw), BF16)],
        compiler_params=_cparams("parallel"),
        name="mla_prep",
    )(small, qnw, kvnw, wuq, wukv, ct, st)


def _mla_kernel(q_ref, k_ref, vt_ref, o_ref, mask_ref, s_ref, mx_ref, *stat_refs, tile, heads):
    i = pl.program_id(2)
    stats = [stat_refs[2 * hh:2 * hh + 2] for hh in range(heads)]
    for st in stats:
        _init_stats(*st)
    key = lax.broadcasted_iota(jnp.int32, (tile, tile), 0)
    qry = lax.broadcasted_iota(jnp.int32, (tile, tile), 1)
    mask_ref[0] = jnp.where(qry >= key, 0.0, NEG_INF)
    mask_ref[1] = jnp.zeros((tile, tile), F32)

    def fill(j, slot, hh, cols):
        rows = _key_rows(j, tile)
        head = slice(hh * LANES, (hh + 1) * LANES)
        mask = mask_ref[jnp.minimum(i - j, 1), :, cols]
        _store_scores(_dot_nt(k_ref[0, rows, head], q_ref[0, cols, head]) + mask,
                      s_ref.at[slot, hh], mx_ref.at[slot, hh], cols)

    def consume(j, slot, hh, cols):
        _softmax_update(s_ref.at[slot, hh], mx_ref.at[slot, hh], vt_ref[0, hh, j], *stats[hh], cols)

    _pipelined_tiles(i + 1, heads, tile, fill, consume)
    for pr in range(heads // 2):
        o = jnp.concatenate([_normalized(stats[2 * pr][1], MLA_V_DIM),
                             _normalized(stats[2 * pr + 1][1], MLA_V_DIM)], axis=0)
        o_ref[0, :, pr * LANES:(pr + 1) * LANES] = o.T.astype(o_ref.dtype)


def _mla_attention(q, k, vt, tile):
    b, s, _ = q.shape
    hg = MLA_HEADS_PER_STEP
    groups = MLA_HEADS // hg
    return pl.pallas_call(
        functools.partial(_mla_kernel, tile=tile, heads=hg),
        grid=(b, groups, s // tile),
        in_specs=[
            pl.BlockSpec((1, tile, hg * LANES), lambda bi, h, i: (bi, i, h)),
            pl.BlockSpec((1, s, hg * LANES), lambda bi, h, i: (bi, 0, h)),
            pl.BlockSpec((1, hg, s // tile, MLA_V_DIM + SUM_ROWS, tile), lambda bi, h, i: (bi, h, 0, 0, 0)),
        ],
        out_specs=pl.BlockSpec((1, tile, hg * MLA_V_DIM), lambda bi, h, i: (bi, i, h)),
        out_shape=jax.ShapeDtypeStruct((b, s, MLA_HEADS * MLA_V_DIM), BF16),
        scratch_shapes=[pltpu.VMEM((2, tile, tile), F32)] + _score_scratch(hg, tile)
        + hg * _stat_scratch(tile, MLA_V_DIM),
        compiler_params=_cparams("parallel", "parallel", "arbitrary"),
        name="mla_attention",
    )(q, k, vt)


def _merge_kernel(x_ref, oa_ref, ob_ref, oc_ref, g_ref, wb_ref, wo_ref, nw_ref, o_ref):
    d = x_ref.shape[1]
    mixed = jnp.zeros(x_ref.shape, F32)
    for gi, br_ref in enumerate((oa_ref, ob_ref, oc_ref)):
        br = _dot(br_ref[...], wb_ref[gi])
        gate = 1.0 / (1.0 + jnp.exp(-g_ref[:, gi * d:(gi + 1) * d].astype(F32)))
        mixed = mixed + gate * br
    y = _dot(mixed.astype(BF16), wo_ref[...])
    o_ref[...] = x_ref[...] + _rms(y, nw_ref[...])


def _merge(x, oa, ob, oc, g, wb, wo, nw, tm):
    n, d = x.shape
    row = lambda i: (i, 0)
    return pl.pallas_call(
        _merge_kernel,
        grid=(n // tm,),
        in_specs=[
            pl.BlockSpec((tm, d), row),
            pl.BlockSpec((tm, BRANCH_WIDTH), row),
            pl.BlockSpec((tm, BRANCH_WIDTH), row),
            pl.BlockSpec((tm, BRANCH_WIDTH), row),
            pl.BlockSpec((tm, N_BRANCH * d), row),
            pl.BlockSpec(wb.shape, lambda i: (0, 0, 0)),
            pl.BlockSpec(wo.shape, lambda i: (0, 0)),
            pl.BlockSpec((1, d), lambda i: (0, 0)),
        ],
        out_specs=pl.BlockSpec((tm, d), row),
        out_shape=jax.ShapeDtypeStruct((n, d), F32),
        compiler_params=_cparams("parallel"),
        name="branch_merge",
    )(x, oa, ob, oc, g, wb, wo, nw)


def _mlp_kernel(x_ref, nw1_ref, wu_ref, wd_ref, nw2_ref, o_ref, h_ref, acc_ref):
    j = pl.program_id(1)

    @pl.when(j == 0)
    def _():
        h_ref[...] = _rms(x_ref[...], nw1_ref[...]).astype(BF16)
        acc_ref[...] = jnp.zeros(acc_ref.shape, F32)

    u = jnp.square(jnp.maximum(_dot(h_ref[...], wu_ref[...]), 0.0))
    acc_ref[...] += _dot(u.astype(BF16), wd_ref[...])

    @pl.when(j == pl.num_programs(1) - 1)
    def _():
        o_ref[...] = x_ref[...] + _rms(acc_ref[...], nw2_ref[...])


def _mlp(x, nw1, wu, wd, nw2, tm, tf):
    n, d = x.shape
    f = wu.shape[1]
    return pl.pallas_call(
        _mlp_kernel,
        grid=(n // tm, f // tf),
        in_specs=[
            pl.BlockSpec((tm, d), lambda i, j: (i, 0)),
            pl.BlockSpec((1, d), lambda i, j: (0, 0)),
            pl.BlockSpec((d, tf), lambda i, j: (0, j)),
            pl.BlockSpec((tf, d), lambda i, j: (j, 0)),
            pl.BlockSpec((1, d), lambda i, j: (0, 0)),
        ],
        out_specs=pl.BlockSpec((tm, d), lambda i, j: (i, 0)),
        out_shape=jax.ShapeDtypeStruct((n, d), F32),
        scratch_shapes=[pltpu.VMEM((tm, d), BF16), pltpu.VMEM((tm, d), F32)],
        compiler_params=_cparams("parallel", "arbitrary"),
        name="relu2_mlp",
    )(x, nw1, wu, wd, nw2)


def _rotate_half_cols(w):
    half = w.shape[-1] // 2
    return jnp.concatenate([-w[..., half:], w[..., :half]], axis=-1)


def _pad_cols(w, before, total):
    pad = [(0, 0)] * (w.ndim - 1) + [(before, total - before - w.shape[-1])]
    return jnp.pad(w, pad)


def _layer_weights(w_in, w_uq, w_ukv):
    n_qkv = 3 * DIFF_HEADS * 2 * DIFF_HEAD_DIM + 3 * MOBA_HEADS * MOBA_HEAD_DIM
    o_cq = n_qkv
    o_ckv = o_cq + MLA_Q_RANK
    o_kpe = o_ckv + MLA_KV_RANK
    o_g = o_kpe + MLA_ROPE_DIM
    w_qkv = w_in[:, :n_qkv].astype(BF16)
    w_kpe = w_in[:, o_kpe:o_g]
    w_small = jnp.concatenate([
        w_in[:, o_cq:o_kpe],
        _pad_cols(w_kpe, MLA_NOPE_DIM, LANES),
        _pad_cols(_rotate_half_cols(w_kpe), MLA_NOPE_DIM, LANES)], axis=1).astype(BF16)
    w_g = w_in[:, o_g:].astype(BF16)

    qk_dim = MLA_NOPE_DIM + MLA_ROPE_DIM
    uq = w_uq.reshape(MLA_Q_RANK, MLA_HEADS, qk_dim)
    uq_plain = _pad_cols(uq, 0, LANES).reshape(MLA_Q_RANK, MLA_HEADS * LANES)
    uq_rot = _pad_cols(_rotate_half_cols(uq[..., MLA_NOPE_DIM:]), MLA_NOPE_DIM, LANES)
    wuq = jnp.concatenate([uq_plain, uq_rot.reshape(MLA_Q_RANK, MLA_HEADS * LANES)], axis=1).astype(BF16)

    ukv = w_ukv.reshape(MLA_KV_RANK, MLA_HEADS, MLA_NOPE_DIM + MLA_V_DIM)
    uk = _pad_cols(ukv[..., :MLA_NOPE_DIM], 0, LANES).reshape(MLA_KV_RANK, MLA_HEADS * LANES)
    uv = ukv[..., MLA_NOPE_DIM:].reshape(MLA_KV_RANK, MLA_HEADS * MLA_V_DIM)
    wukv = jnp.concatenate([uk, uv], axis=1).astype(BF16)
    return w_qkv, w_small, w_g, wuq, wukv


def kernel(x, positions, rel_bias, norm_mix_pre, norm_mix_post, norm_mlp_pre, norm_mlp_post, w_in, diff_lambda, diff_subln, mla_q_norm, mla_w_uq, mla_kv_norm, mla_w_ukv, w_branch, w_out, w_up, w_down):
    b, s, d = x.shape
    n = b * s
    depth = w_in.shape[0]
    tile = ATTN_TILE
    assert tile % MOBA_BLOCK == 0 and s % tile == 0 and n % 2048 == 0 and (s // MOBA_BLOCK) % 8 == 0

    xf = x.reshape(n, d)
    bias = _bias_tiles(rel_bias, tile)

    half = MLA_ROPE_DIM // 2
    inv_freq = ROPE_THETA ** (-jnp.arange(half, dtype=F32) * 2.0 / MLA_ROPE_DIM)
    ang = positions.astype(F32).reshape(n, 1) * inv_freq
    cos, sin = jnp.cos(ang), jnp.sin(ang)
    ct = jnp.concatenate([jnp.ones((n, MLA_NOPE_DIM), F32), cos, cos,
                          jnp.zeros((n, LANES - MLA_NOPE_DIM - MLA_ROPE_DIM), F32)], axis=1)
    st = jnp.concatenate([jnp.zeros((n, MLA_NOPE_DIM), F32), sin, sin,
                          jnp.zeros((n, LANES - MLA_NOPE_DIM - MLA_ROPE_DIM), F32)], axis=1)

    n_diff = DIFF_HEADS * 2 * DIFF_HEAD_DIM
    n_moba = MOBA_HEADS * MOBA_HEAD_DIM
    qkv_scale = jnp.concatenate([
        jnp.full((n_diff,), LOG2E / math.sqrt(DIFF_HEAD_DIM), F32), jnp.ones((2 * n_diff,), F32),
        jnp.full((n_moba,), LOG2E / math.sqrt(MOBA_HEAD_DIM), F32), jnp.ones((2 * n_moba,), F32)]).reshape(1, -1)

    row = lambda a: a.reshape(1, -1)
    for l in range(depth):
        lam_init = 0.8 - 0.6 * math.exp(-0.3 * l)
        w_qkv, w_small, w_g, wuq, wukv = _layer_weights(w_in[l], mla_w_uq[l], mla_w_ukv[l])
        nw = row(norm_mix_pre[l])
        qkv = _norm_matmul(xf, nw, w_qkv, qkv_scale, BF16, 2048, 1024)
        small = _norm_matmul(xf, nw, w_small, jnp.ones((1, w_small.shape[1]), F32), F32, 2048, w_small.shape[1])
        g = _norm_matmul(xf, nw, w_g, jnp.ones((1, w_g.shape[1]), F32), BF16, 2048, 1024)

        qkv3 = qkv.reshape(b, s, -1)
        scal = jnp.full((1,), lam_init, F32)
        diff_vt = _transposed_values(qkv3[:, :, 2 * n_diff:3 * n_diff], DIFF_HEADS, tile)
        oa = _diff_attention(qkv3, diff_vt, bias, scal, diff_lambda[l], row(diff_subln[l]), tile)
        moba_vt = _transposed_values(qkv3[:, :, 3 * n_diff + 2 * n_moba:], MOBA_HEADS, tile)
        ob = _moba_attention(qkv3, moba_vt, bias, tile)
        mq, mk, mv = _mla_prep(small, row(mla_q_norm[l]), row(mla_kv_norm[l]), wuq, wukv, ct, st, 512)
        mla_vt = _transposed_values(mv.reshape(b, s, -1), MLA_HEADS, tile)
        oc = _mla_attention(mq.reshape(b, s, -1), mk.reshape(b, s, -1), mla_vt, tile)

        xf = _merge(xf, oa.reshape(n, -1), ob.reshape(n, -1), oc.reshape(n, -1), g,
                    w_branch[l].astype(BF16), w_out[l].astype(BF16), row(norm_mix_post[l]), 512)
        xf = _mlp(xf, row(norm_mlp_pre[l]), w_up[l].astype(BF16), w_down[l].astype(BF16),
                  row(norm_mlp_post[l]), 1024, 512)
    return xf.reshape(b, s, d)
```

```python
import functools
import itertools
import math

import jax
import jax.numpy as jnp
from jax import lax
from jax.experimental import pallas as pl
from jax.experimental.pallas import tpu as pltpu

F32 = jnp.float32
BF16 = jnp.bfloat16

RMS_EPS = 1e-6
NEG_INF = -1e30
STAT_INIT = -(2.0 ** 100)

T5_BUCKETS = 32
T5_MAX_EXACT = T5_BUCKETS // 2
T5_MAX_DISTANCE = 1024

DIFF_HEADS = 4
DIFF_HEAD_DIM = 64
MOBA_HEADS = 4
MOBA_HEAD_DIM = 128
MOBA_BLOCK = 256
MOBA_TOPK = 3
MLA_HEADS = 8
MLA_Q_RANK = 256
MLA_KV_RANK = 128
MLA_NOPE_DIM = 64
MLA_ROPE_DIM = 32
MLA_V_DIM = 64
ROPE_THETA = 10000.0
N_BRANCH = 3
BRANCH_WIDTH = 512

LOG2E = math.log2(math.e)
MOBA_HEADS_PER_STEP = 2
MLA_HEADS_PER_STEP = 4
PIPELINE_UNROLL = 4
SUM_ROWS = 16
LANES = 128
ATTN_TILE = 512
VMEM_LIMIT = 48 * 1024 * 1024


def _cparams(*sem):
    return pltpu.CompilerParams(dimension_semantics=sem, vmem_limit_bytes=VMEM_LIMIT)


def _rms(x, w):
    return x * lax.rsqrt(jnp.mean(x * x, axis=-1, keepdims=True) + RMS_EPS) * w


def _dot(a, b):
    return jnp.dot(a, b, preferred_element_type=F32)


def _dot_nt(a, b):
    return lax.dot_general(a, b, (((1,), (1,)), ((), ())), preferred_element_type=F32)


def _norm_matmul_kernel(x_ref, nw_ref, w_ref, cs_ref, o_ref, h_ref):
    @pl.when(pl.program_id(1) == 0)
    def _():
        h_ref[...] = _rms(x_ref[...], nw_ref[...]).astype(BF16)

    o_ref[...] = (_dot(h_ref[...], w_ref[...]) * cs_ref[...]).astype(o_ref.dtype)


def _norm_matmul(x, nw, w, colscale, out_dtype, tm, tn):
    n, d = x.shape
    nout = w.shape[1]
    return pl.pallas_call(
        _norm_matmul_kernel,
        grid=(n // tm, nout // tn),
        in_specs=[
            pl.BlockSpec((tm, d), lambda i, j: (i, 0)),
            pl.BlockSpec((1, d), lambda i, j: (0, 0)),
            pl.BlockSpec((d, tn), lambda i, j: (0, j)),
            pl.BlockSpec((1, tn), lambda i, j: (0, j)),
        ],
        out_specs=pl.BlockSpec((tm, tn), lambda i, j: (i, j)),
        out_shape=jax.ShapeDtypeStruct((n, nout), out_dtype),
        scratch_shapes=[pltpu.VMEM((tm, d), BF16)],
        compiler_params=_cparams("parallel", "arbitrary"),
        name="norm_matmul",
    )(x, nw, w, colscale)


def _bias_tile_kernel(tab_ref, o_ref, *, tile):
    h = pl.program_id(0)
    d = pl.program_id(1)
    c = lax.broadcasted_iota(jnp.int32, (tile, tile), 0)
    r = lax.broadcasted_iota(jnp.int32, (tile, tile), 1)
    rel = d * tile + r - c
    n = jnp.maximum(rel, 0)
    nf = jnp.maximum(n, 1).astype(F32)
    large = T5_MAX_EXACT + (jnp.log(nf / T5_MAX_EXACT)
                            / math.log(T5_MAX_DISTANCE / T5_MAX_EXACT)
                            * (T5_BUCKETS - T5_MAX_EXACT)).astype(jnp.int32)
    large = jnp.minimum(large, T5_BUCKETS - 1)
    bucket = jnp.where(n < T5_MAX_EXACT, n, large)
    val = jnp.zeros((tile, tile), F32)
    for b in range(T5_BUCKETS):
        val = jnp.where(bucket == b, tab_ref[b, h], val)
    o_ref[0, 0] = jnp.where(rel >= 0, val * LOG2E, NEG_INF)


def _num_bias_tiles(tile):
    return -(-(T5_MAX_DISTANCE + tile - 1) // tile) + 1


def _bias_tiles(rel_bias, tile):
    nh = rel_bias.shape[1]
    nd = _num_bias_tiles(tile)
    return pl.pallas_call(
        functools.partial(_bias_tile_kernel, tile=tile),
        grid=(nh, nd),
        in_specs=[pl.BlockSpec(memory_space=pltpu.SMEM)],
        out_specs=pl.BlockSpec((1, 1, tile, tile), lambda h, d: (h, d, 0, 0)),
        out_shape=jax.ShapeDtypeStruct((nh, nd, tile, tile), F32),
        compiler_params=_cparams("parallel", "parallel"),
        name="t5_bias_tiles",
    )(rel_bias)


def _store_scores(s, s_ref, mx_ref):
    s_ref[...] = s.astype(BF16)
    mx_ref[...] = jnp.max(s, axis=0, keepdims=True).astype(BF16).astype(F32)


def _softmax_update(s_ref, mx_ref, vt, m_ref, acc_ref):
    m_prev = m_ref[...]
    m_new = jnp.maximum(m_prev, mx_ref[...])
    alpha = jnp.exp2(m_prev - m_new)
    p = jnp.exp2(s_ref[...] - m_new.astype(BF16))
    acc_ref[...] = alpha * acc_ref[...] + _dot(vt, p)
    m_ref[...] = m_new


def _paired_pipeline(i, nq, n_maps, fill, consume):
    def item(t):
        second = (jnp.int32(t) > i).astype(jnp.int32)
        j = t - second * (i + 1)
        dist = i + second * (nq - 1 - 2 * i) - j
        return second, j, dist

    def fill_item(t, slot):
        qt, j, dist = item(t)
        for mp in range(n_maps):
            fill(qt, j, dist, slot, mp)

    def step(t, slot):
        qt_f, j_f, dist_f = item(t + 1)
        qt_u, j_u, _ = item(t)
        for mp in range(n_maps):
            fill(qt_f, j_f, dist_f, 1 - slot, mp)
            consume(qt_u, j_u, slot, mp)

    fill_item(0, 0)

    def trip(r, c):
        for u in range(PIPELINE_UNROLL):
            step(r * PIPELINE_UNROLL + u, u % 2)
        return c

    lax.fori_loop(0, nq // PIPELINE_UNROLL, trip, 0)
    qt, j, _ = item(nq)
    for mp in range(n_maps):
        consume(qt, j, 0, mp)


def _init_stats(m_ref, acc_ref):
    m_ref[...] = jnp.full(m_ref.shape, STAT_INIT, F32)
    acc_ref[...] = jnp.zeros(acc_ref.shape, F32)


def _stat_scratch(tile, dv):
    return [pltpu.VMEM((2, 1, tile), F32), pltpu.VMEM((2, dv + SUM_ROWS, tile), F32)]


def _score_scratch(n_maps, tile):
    return [pltpu.VMEM((2, n_maps, tile, tile), BF16), pltpu.VMEM((2, n_maps, 1, tile), F32)]


def _normalized(acc_ref, qt, dv):
    return acc_ref[qt, :dv, :] / acc_ref[qt, dv:dv + 1, :]


def _paired_query_specs(tile, width, nq, col_of):
    return [pl.BlockSpec((1, tile, width), lambda bi, h, i: (bi, i, col_of(h))),
            pl.BlockSpec((1, tile, width), lambda bi, h, i: (bi, nq - 1 - i, col_of(h)))]


def _paired_outputs(b, s, tile, width, total_width):
    half = s // tile // 2
    specs = [pl.BlockSpec((1, tile, width), lambda bi, h, i: (bi, i, h)),
             pl.BlockSpec((1, tile, width), lambda bi, h, i: (bi, half - 1 - i, h))]
    shapes = [jax.ShapeDtypeStruct((b, s // 2, total_width), BF16)] * 2
    return specs, shapes


def _key_rows(j, tile):
    return pl.ds(pl.multiple_of(j * tile, tile), tile)


def _transposed_values(v3, groups, tile):
    b, s, w = v3.shape
    v = v3.reshape(b, s // tile, tile, groups, w // groups)
    v = jnp.concatenate([v, jnp.ones(v.shape[:-1] + (SUM_ROWS,), v.dtype)], axis=-1)
    return v.transpose(0, 3, 1, 4, 2)


def _diff_kernel(scal_ref, lam_ref, subw_ref, qa_ref, qb_ref, k_ref, vt_ref, bias_ref, oa_ref, ob_ref,
                 qm_ref, s_ref, mx_ref, m1, a1, m2, a2, *, tile, nd, nq):
    i = pl.program_id(2)
    for qt, q_ref in enumerate((qa_ref, qb_ref)):
        q = q_ref[0]
        lane = lax.broadcasted_iota(jnp.int32, q.shape, 1)
        zero = jnp.zeros_like(q)
        qm_ref[qt, 0] = jnp.where(lane < DIFF_HEAD_DIM, q, zero)
        qm_ref[qt, 1] = jnp.where(lane >= DIFF_HEAD_DIM, q, zero)
    stats = ((m1, a1), (m2, a2))
    for st in stats:
        _init_stats(*st)

    def fill(qt, j, dist, slot, mp):
        k = k_ref[0, _key_rows(j, tile), :]
        bias = bias_ref[0, jnp.minimum(dist, nd - 1)]
        _store_scores(_dot_nt(k, qm_ref[qt, mp]) + bias, s_ref.at[slot, mp], mx_ref.at[slot, mp])

    def consume(qt, j, slot, mp):
        m_ref, acc_ref = stats[mp]
        _softmax_update(s_ref.at[slot, mp], mx_ref.at[slot, mp], vt_ref[0, 0, j], m_ref.at[qt], acc_ref.at[qt])

    _paired_pipeline(i, nq, 2, fill, consume)

    lam_init = scal_ref[0]
    lv = lam_ref[...]
    lam = (jnp.exp(jnp.sum(lv[0:1] * lv[1:2], axis=1, keepdims=True))
           - jnp.exp(jnp.sum(lv[2:3] * lv[3:4], axis=1, keepdims=True)) + lam_init)
    hd = 2 * DIFF_HEAD_DIM
    for qt, o_ref in enumerate((oa_ref, ob_ref)):
        o = (_normalized(a1, qt, hd) - lam * _normalized(a2, qt, hd)).T
        o_ref[0] = (_rms(o, subw_ref[...]) * (1.0 - lam_init)).astype(o_ref.dtype)


def _diff_attention(qkv, vt, bias, scal, lam_vecs, subw, tile):
    b, s, _ = qkv.shape
    nd = bias.shape[1]
    nq = s // tile
    hd = 2 * DIFF_HEAD_DIM
    out_specs, out_shapes = _paired_outputs(b, s, tile, hd, DIFF_HEADS * hd)
    lo, hi = pl.pallas_call(
        functools.partial(_diff_kernel, tile=tile, nd=nd, nq=nq),
        grid=(b, DIFF_HEADS, nq // 2),
        in_specs=[
            pl.BlockSpec(memory_space=pltpu.SMEM),
            pl.BlockSpec(lam_vecs.shape, lambda bi, h, i: (0, 0)),
            pl.BlockSpec((1, hd), lambda bi, h, i: (0, 0)),
            *_paired_query_specs(tile, hd, nq, lambda h: h),
            pl.BlockSpec((1, s, hd), lambda bi, h, i: (bi, 0, DIFF_HEADS + h)),
            pl.BlockSpec((1, 1, nq, hd + SUM_ROWS, tile), lambda bi, h, i: (bi, h, 0, 0, 0)),
            pl.BlockSpec((1, nd, tile, tile), lambda bi, h, i: (h, 0, 0, 0)),
        ],
        out_specs=out_specs,
        out_shape=out_shapes,
        scratch_shapes=[pltpu.VMEM((2, 2, tile, hd), BF16)] + _score_scratch(2, tile)
        + _stat_scratch(tile, hd) + _stat_scratch(tile, hd),
        compiler_params=_cparams("parallel", "parallel", "arbitrary"),
        name="diff_attention",
    )(scal, lam_vecs, subw, qkv, qkv, qkv, vt, bias)
    return jnp.concatenate([lo, hi], axis=1)


def _moba_kernel(qa_ref, qb_ref, k_ref, oh_ref, vt_ref, bias_ref, oa_ref, ob_ref,
                 kmean_ref, qaug_ref, s_ref, mx_ref, *stat_refs, tile, nd, nblk, heads, nq):
    blk = MOBA_BLOCK
    hd = MOBA_HEAD_DIM
    per_tile = tile // blk
    i = pl.program_id(2)
    stats = [stat_refs[2 * hh:2 * hh + 2] for hh in range(heads)]

    @pl.when(i == 0)
    def _():
        for hh in range(heads):
            for n in range(nblk):
                kb = k_ref[0, n * blk:(n + 1) * blk, hh * hd:(hh + 1) * hd].astype(F32)
                mean = jnp.mean(kb, axis=0, keepdims=True)
                for part in range(3):
                    term = mean.astype(BF16).astype(F32)
                    kmean_ref[hh, part * nblk + n:part * nblk + n + 1, :] = term
                    mean = mean - term

    for hh, (qt, q_ref) in itertools.product(range(heads), enumerate((qa_ref, qb_ref))):
        q = q_ref[0, :, hh * hd:(hh + 1) * hd]
        q_tile = i if qt == 0 else nq - 1 - i
        parts = _dot_nt(kmean_ref[hh].astype(BF16), q)
        gate = parts[:nblk] + parts[nblk:2 * nblk] + parts[2 * nblk:]
        blk_id = lax.broadcasted_iota(jnp.int32, gate.shape, 0)
        blk_f = blk_id.astype(F32)
        lane = lax.broadcasted_iota(jnp.int32, gate.shape, 1)
        own = q_tile * per_tile + lax.shift_right_logical(lane, int(math.log2(blk)))
        valid = blk_id < own
        gate = jnp.where(valid, gate, NEG_INF)
        sel = blk_id == own
        for _ in range(MOBA_TOPK):
            top = jnp.max(gate, axis=0, keepdims=True)
            first = jnp.min(jnp.where(gate == top, blk_f, float(nblk)), axis=0, keepdims=True)
            hit = blk_f == first
            sel = jnp.logical_or(sel, jnp.logical_and(hit, valid))
            gate = jnp.where(hit, -3.0e38, gate)
        pen = jnp.where(sel, 0.0, NEG_INF)
        pen = jnp.concatenate([pen, jnp.zeros((LANES - nblk, tile), F32)], axis=0)
        qaug_ref[qt, hh, :, :LANES] = q
        qaug_ref[qt, hh, :, LANES:] = pen.T.astype(BF16)
    for st in stats:
        _init_stats(*st)

    def fill(qt, j, dist, slot, hh):
        rows = _key_rows(j, tile)
        kaug = jnp.concatenate([k_ref[0, rows, hh * hd:(hh + 1) * hd], oh_ref[rows, :]], axis=1)
        bias = bias_ref[hh, jnp.minimum(dist, nd - 1)]
        _store_scores(_dot_nt(kaug, qaug_ref[qt, hh]) + bias, s_ref.at[slot, hh], mx_ref.at[slot, hh])

    def consume(qt, j, slot, hh):
        m_ref, acc_ref = stats[hh]
        _softmax_update(s_ref.at[slot, hh], mx_ref.at[slot, hh], vt_ref[0, hh, j], m_ref.at[qt], acc_ref.at[qt])

    _paired_pipeline(i, nq, heads, fill, consume)
    for hh, (qt, o_ref) in itertools.product(range(heads), enumerate((oa_ref, ob_ref))):
        o_ref[0, :, hh * hd:(hh + 1) * hd] = _normalized(stats[hh][1], qt, hd).T.astype(o_ref.dtype)


def _moba_attention(qkv, vt, bias, tile):
    b, s, _ = qkv.shape
    nd = bias.shape[1]
    nblk = s // MOBA_BLOCK
    hd = MOBA_HEAD_DIM
    hg = MOBA_HEADS_PER_STEP
    w = hg * hd
    col0 = 3 * DIFF_HEADS // hg
    groups = MOBA_HEADS // hg
    onehot = (jnp.arange(s)[:, None] // MOBA_BLOCK == jnp.arange(LANES)[None, :]).astype(BF16)
    nq = s // tile
    out_specs, out_shapes = _paired_outputs(b, s, tile, w, MOBA_HEADS * hd)
    lo, hi = pl.pallas_call(
        functools.partial(_moba_kernel, tile=tile, nd=nd, nblk=nblk, heads=hg, nq=nq),
        grid=(b, groups, nq // 2),
        in_specs=[
            *_paired_query_specs(tile, w, nq, lambda h: col0 + h),
            pl.BlockSpec((1, s, w), lambda bi, h, i: (bi, 0, col0 + groups + h)),
            pl.BlockSpec((s, LANES), lambda bi, h, i: (0, 0)),
            pl.BlockSpec((1, hg, nq, hd + SUM_ROWS, tile), lambda bi, h, i: (bi, h, 0, 0, 0)),
            pl.BlockSpec((hg, nd, tile, tile), lambda bi, h, i: (DIFF_HEADS // hg + h, 0, 0, 0)),
        ],
        out_specs=out_specs,
        out_shape=out_shapes,
        scratch_shapes=[pltpu.VMEM((hg, 3 * nblk, hd), F32), pltpu.VMEM((2, hg, tile, 2 * LANES), BF16)]
        + _score_scratch(hg, tile) + hg * _stat_scratch(tile, hd),
        compiler_params=_cparams("parallel", "parallel", "arbitrary"),
        name="moba_attention",
    )(qkv, qkv, qkv, onehot, vt, bias)
    return jnp.concatenate([lo, hi], axis=1)


def _mla_prep_kernel(small_ref, qnw_ref, kvnw_ref, wuq_ref, wukv_ref, ct_ref, st_ref,
                     q_ref, k_ref, v_ref, *, scale):
    kw = MLA_HEADS * LANES
    ct = ct_ref[...]
    st = st_ref[...]
    cq = small_ref[:, :MLA_Q_RANK]
    ckv = small_ref[:, MLA_Q_RANK:MLA_Q_RANK + MLA_KV_RANK]
    kpe = small_ref[:, MLA_Q_RANK + MLA_KV_RANK:MLA_Q_RANK + MLA_KV_RANK + LANES]
    kpe_rot = small_ref[:, MLA_Q_RANK + MLA_KV_RANK + LANES:]
    qq = _dot(_rms(cq, qnw_ref[...]).astype(BF16), wuq_ref[...])
    kk = _dot(_rms(ckv, kvnw_ref[...]).astype(BF16), wukv_ref[...])
    k_rope = kpe * ct + kpe_rot * st
    for h in range(MLA_HEADS):
        cols = slice(h * LANES, (h + 1) * LANES)
        rot = qq[:, kw + h * LANES:kw + (h + 1) * LANES]
        q_ref[:, cols] = ((qq[:, cols] * ct + rot * st) * scale).astype(BF16)
        k_ref[:, cols] = (kk[:, cols] + k_rope).astype(BF16)
    v_ref[...] = kk[:, kw:].astype(BF16)


def _mla_prep(small, qnw, kvnw, wuq, wukv, ct, st, tm):
    n = small.shape[0]
    kw = MLA_HEADS * LANES
    vw = MLA_HEADS * MLA_V_DIM
    scale = LOG2E / math.sqrt(MLA_NOPE_DIM + MLA_ROPE_DIM)
    row = lambda i: (i, 0)
    const = lambda i: (0, 0)
    return pl.pallas_call(
        functools.partial(_mla_prep_kernel, scale=scale),
        grid=(n // tm,),
        in_specs=[
            pl.BlockSpec((tm, small.shape[1]), row),
            pl.BlockSpec(qnw.shape, const),
            pl.BlockSpec(kvnw.shape, const),
            pl.BlockSpec(wuq.shape, const),
            pl.BlockSpec(wukv.shape, const),
            pl.BlockSpec((tm, LANES), row),
            pl.BlockSpec((tm, LANES), row),
        ],
        out_specs=[pl.BlockSpec((tm, kw), row), pl.BlockSpec((tm, kw), row), pl.BlockSpec((tm, vw), row)],
        out_shape=[jax.ShapeDtypeStruct((n, kw), BF16), jax.ShapeDtypeStruct((n, kw), BF16),
                   jax.ShapeDtypeStruct((n, vw), BF16)],
        compiler_params=_cparams("parallel"),
        name="mla_prep",
    )(small, qnw, kvnw, wuq, wukv, ct, st)


def _mla_kernel(qa_ref, qb_ref, k_ref, vt_ref, oa_ref, ob_ref, q_ref, mask_ref, s_ref, mx_ref, *stat_refs,
                tile, heads, nq):
    i = pl.program_id(2)
    stats = [stat_refs[2 * hh:2 * hh + 2] for hh in range(heads)]
    for st in stats:
        _init_stats(*st)
    q_ref[0] = qa_ref[0]
    q_ref[1] = qb_ref[0]
    key = lax.broadcasted_iota(jnp.int32, (tile, tile), 0)
    qry = lax.broadcasted_iota(jnp.int32, (tile, tile), 1)
    mask_ref[0] = jnp.where(qry >= key, 0.0, NEG_INF)
    mask_ref[1] = jnp.zeros((tile, tile), F32)

    def fill(qt, j, dist, slot, hh):
        rows = _key_rows(j, tile)
        head = slice(hh * LANES, (hh + 1) * LANES)
        mask = mask_ref[jnp.minimum(dist, 1)]
        _store_scores(_dot_nt(k_ref[0, rows, head], q_ref[qt, :, head]) + mask,
                      s_ref.at[slot, hh], mx_ref.at[slot, hh])

    def consume(qt, j, slot, hh):
        m_ref, acc_ref = stats[hh]
        _softmax_update(s_ref.at[slot, hh], mx_ref.at[slot, hh], vt_ref[0, hh, j], m_ref.at[qt], acc_ref.at[qt])

    _paired_pipeline(i, nq, heads, fill, consume)
    for pr, (qt, o_ref) in itertools.product(range(heads // 2), enumerate((oa_ref, ob_ref))):
        o = jnp.concatenate([_normalized(stats[2 * pr][1], qt, MLA_V_DIM),
                             _normalized(stats[2 * pr + 1][1], qt, MLA_V_DIM)], axis=0)
        o_ref[0, :, pr * LANES:(pr + 1) * LANES] = o.T.astype(o_ref.dtype)


def _mla_attention(q, k, vt, tile):
    b, s, _ = q.shape
    hg = MLA_HEADS_PER_STEP
    groups = MLA_HEADS // hg
    nq = s // tile
    out_specs, out_shapes = _paired_outputs(b, s, tile, hg * MLA_V_DIM, MLA_HEADS * MLA_V_DIM)
    lo, hi = pl.pallas_call(
        functools.partial(_mla_kernel, tile=tile, heads=hg, nq=nq),
        grid=(b, groups, nq // 2),
        in_specs=[
            *_paired_query_specs(tile, hg * LANES, nq, lambda h: h),
            pl.BlockSpec((1, s, hg * LANES), lambda bi, h, i: (bi, 0, h)),
            pl.BlockSpec((1, hg, nq, MLA_V_DIM + SUM_ROWS, tile), lambda bi, h, i: (bi, h, 0, 0, 0)),
        ],
        out_specs=out_specs,
        out_shape=out_shapes,
        scratch_shapes=[pltpu.VMEM((2, tile, hg * LANES), BF16), pltpu.VMEM((2, tile, tile), F32)]
        + _score_scratch(hg, tile) + hg * _stat_scratch(tile, MLA_V_DIM),
        compiler_params=_cparams("parallel", "parallel", "arbitrary"),
        name="mla_attention",
    )(q, q, k, vt)
    return jnp.concatenate([lo, hi], axis=1)


def _merge_kernel(x_ref, oa_ref, ob_ref, oc_ref, g_ref, wb_ref, wo_ref, nw_ref, o_ref):
    d = x_ref.shape[1]
    mixed = jnp.zeros(x_ref.shape, F32)
    for gi, br_ref in enumerate((oa_ref, ob_ref, oc_ref)):
        br = _dot(br_ref[...], wb_ref[gi])
        gate = 1.0 / (1.0 + jnp.exp(-g_ref[:, gi * d:(gi + 1) * d].astype(F32)))
        mixed = mixed + gate * br
    y = _dot(mixed.astype(BF16), wo_ref[...])
    o_ref[...] = x_ref[...] + _rms(y, nw_ref[...])


def _merge(x, oa, ob, oc, g, wb, wo, nw, tm):
    n, d = x.shape
    row = lambda i: (i, 0)
    return pl.pallas_call(
        _merge_kernel,
        grid=(n // tm,),
        in_specs=[
            pl.BlockSpec((tm, d), row),
            pl.BlockSpec((tm, BRANCH_WIDTH), row),
            pl.BlockSpec((tm, BRANCH_WIDTH), row),
            pl.BlockSpec((tm, BRANCH_WIDTH), row),
            pl.BlockSpec((tm, N_BRANCH * d), row),
            pl.BlockSpec(wb.shape, lambda i: (0, 0, 0)),
            pl.BlockSpec(wo.shape, lambda i: (0, 0)),
            pl.BlockSpec((1, d), lambda i: (0, 0)),
        ],
        out_specs=pl.BlockSpec((tm, d), row),
        out_shape=jax.ShapeDtypeStruct((n, d), F32),
        compiler_params=_cparams("parallel"),
        name="branch_merge",
    )(x, oa, ob, oc, g, wb, wo, nw)


def _mlp_kernel(x_ref, nw1_ref, wu_ref, wd_ref, nw2_ref, o_ref, h_ref, acc_ref):
    j = pl.program_id(1)

    @pl.when(j == 0)
    def _():
        h_ref[...] = _rms(x_ref[...], nw1_ref[...]).astype(BF16)
        acc_ref[...] = jnp.zeros(acc_ref.shape, F32)

    u = jnp.square(jnp.maximum(_dot(h_ref[...], wu_ref[...]), 0.0))
    acc_ref[...] += _dot(u.astype(BF16), wd_ref[...])

    @pl.when(j == pl.num_programs(1) - 1)
    def _():
        o_ref[...] = x_ref[...] + _rms(acc_ref[...], nw2_ref[...])


def _mlp(x, nw1, wu, wd, nw2, tm, tf):
    n, d = x.shape
    f = wu.shape[1]
    return pl.pallas_call(
        _mlp_kernel,
        grid=(n // tm, f // tf),
        in_specs=[
            pl.BlockSpec((tm, d), lambda i, j: (i, 0)),
            pl.BlockSpec((1, d), lambda i, j: (0, 0)),
            pl.BlockSpec((d, tf), lambda i, j: (0, j)),
            pl.BlockSpec((tf, d), lambda i, j: (j, 0)),
            pl.BlockSpec((1, d), lambda i, j: (0, 0)),
        ],
        out_specs=pl.BlockSpec((tm, d), lambda i, j: (i, 0)),
        out_shape=jax.ShapeDtypeStruct((n, d), F32),
        scratch_shapes=[pltpu.VMEM((tm, d), BF16), pltpu.VMEM((tm, d), F32)],
        compiler_params=_cparams("parallel", "arbitrary"),
        name="relu2_mlp",
    )(x, nw1, wu, wd, nw2)


def _rotate_half_cols(w):
    half = w.shape[-1] // 2
    return jnp.concatenate([-w[..., half:], w[..., :half]], axis=-1)


def _pad_cols(w, before, total):
    pad = [(0, 0)] * (w.ndim - 1) + [(before, total - before - w.shape[-1])]
    return jnp.pad(w, pad)


def _layer_weights(w_in, w_uq, w_ukv):
    n_qkv = 3 * DIFF_HEADS * 2 * DIFF_HEAD_DIM + 3 * MOBA_HEADS * MOBA_HEAD_DIM
    o_cq = n_qkv
    o_ckv = o_cq + MLA_Q_RANK
    o_kpe = o_ckv + MLA_KV_RANK
    o_g = o_kpe + MLA_ROPE_DIM
    w_qkv = w_in[:, :n_qkv].astype(BF16)
    w_kpe = w_in[:, o_kpe:o_g]
    w_small = jnp.concatenate([
        w_in[:, o_cq:o_kpe],
        _pad_cols(w_kpe, MLA_NOPE_DIM, LANES),
        _pad_cols(_rotate_half_cols(w_kpe), MLA_NOPE_DIM, LANES)], axis=1).astype(BF16)
    w_g = w_in[:, o_g:].astype(BF16)

    qk_dim = MLA_NOPE_DIM + MLA_ROPE_DIM
    uq = w_uq.reshape(MLA_Q_RANK, MLA_HEADS, qk_dim)
    uq_plain = _pad_cols(uq, 0, LANES).reshape(MLA_Q_RANK, MLA_HEADS * LANES)
    uq_rot = _pad_cols(_rotate_half_cols(uq[..., MLA_NOPE_DIM:]), MLA_NOPE_DIM, LANES)
    wuq = jnp.concatenate([uq_plain, uq_rot.reshape(MLA_Q_RANK, MLA_HEADS * LANES)], axis=1).astype(BF16)

    ukv = w_ukv.reshape(MLA_KV_RANK, MLA_HEADS, MLA_NOPE_DIM + MLA_V_DIM)
    uk = _pad_cols(ukv[..., :MLA_NOPE_DIM], 0, LANES).reshape(MLA_KV_RANK, MLA_HEADS * LANES)
    uv = ukv[..., MLA_NOPE_DIM:].reshape(MLA_KV_RANK, MLA_HEADS * MLA_V_DIM)
    wukv = jnp.concatenate([uk, uv], axis=1).astype(BF16)
    return w_qkv, w_small, w_g, wuq, wukv


def kernel(x, positions, rel_bias, norm_mix_pre, norm_mix_post, norm_mlp_pre, norm_mlp_post, w_in, diff_lambda, diff_subln, mla_q_norm, mla_w_uq, mla_kv_norm, mla_w_ukv, w_branch, w_out, w_up, w_down):
    b, s, d = x.shape
    n = b * s
    depth = w_in.shape[0]
    tile = ATTN_TILE
    assert tile % MOBA_BLOCK == 0 and n % 2048 == 0 and (s // MOBA_BLOCK) % 8 == 0
    assert s % (tile * PIPELINE_UNROLL) == 0 and PIPELINE_UNROLL % 2 == 0

    xf = x.reshape(n, d)
    bias = _bias_tiles(rel_bias, tile)

    half = MLA_ROPE_DIM // 2
    inv_freq = ROPE_THETA ** (-jnp.arange(half, dtype=F32) * 2.0 / MLA_ROPE_DIM)
    ang = positions.astype(F32).reshape(n, 1) * inv_freq
    cos, sin = jnp.cos(ang), jnp.sin(ang)
    ct = jnp.concatenate([jnp.ones((n, MLA_NOPE_DIM), F32), cos, cos,
                          jnp.zeros((n, LANES - MLA_NOPE_DIM - MLA_ROPE_DIM), F32)], axis=1)
    st = jnp.concatenate([jnp.zeros((n, MLA_NOPE_DIM), F32), sin, sin,
                          jnp.zeros((n, LANES - MLA_NOPE_DIM - MLA_ROPE_DIM), F32)], axis=1)

    n_diff = DIFF_HEADS * 2 * DIFF_HEAD_DIM
    n_moba = MOBA_HEADS * MOBA_HEAD_DIM
    qkv_scale = jnp.concatenate([
        jnp.full((n_diff,), LOG2E / math.sqrt(DIFF_HEAD_DIM), F32), jnp.ones((2 * n_diff,), F32),
        jnp.full((n_moba,), LOG2E / math.sqrt(MOBA_HEAD_DIM), F32), jnp.ones((2 * n_moba,), F32)]).reshape(1, -1)

    row = lambda a: a.reshape(1, -1)
    for l in range(depth):
        lam_init = 0.8 - 0.6 * math.exp(-0.3 * l)
        w_qkv, w_small, w_g, wuq, wukv = _layer_weights(w_in[l], mla_w_uq[l], mla_w_ukv[l])
        nw = row(norm_mix_pre[l])
        qkv = _norm_matmul(xf, nw, w_qkv, qkv_scale, BF16, 2048, 1024)
        small = _norm_matmul(xf, nw, w_small, jnp.ones((1, w_small.shape[1]), F32), F32, 2048, w_small.shape[1])
        g = _norm_matmul(xf, nw, w_g, jnp.ones((1, w_g.shape[1]), F32), BF16, 2048, 1024)

        qkv3 = qkv.reshape(b, s, -1)
        scal = jnp.full((1,), lam_init, F32)
        diff_vt = _transposed_values(qkv3[:, :, 2 * n_diff:3 * n_diff], DIFF_HEADS, tile)
        oa = _diff_attention(qkv3, diff_vt, bias, scal, diff_lambda[l], row(diff_subln[l]), tile)
        moba_vt = _transposed_values(qkv3[:, :, 3 * n_diff + 2 * n_moba:], MOBA_HEADS, tile)
        ob = _moba_attention(qkv3, moba_vt, bias, tile)
        mq, mk, mv = _mla_prep(small, row(mla_q_norm[l]), row(mla_kv_norm[l]), wuq, wukv, ct, st, 512)
        mla_vt = _transposed_values(mv.reshape(b, s, -1), MLA_HEADS, tile)
        oc = _mla_attention(mq.reshape(b, s, -1), mk.reshape(b, s, -1), mla_vt, tile)

        xf = _merge(xf, oa.reshape(n, -1), ob.reshape(n, -1), oc.reshape(n, -1), g,
                    w_branch[l].astype(BF16), w_out[l].astype(BF16), row(norm_mix_post[l]), 512)
        xf = _mlp(xf, row(norm_mlp_pre[l]), w_up[l].astype(BF16), w_down[l].astype(BF16),
                  row(norm_mlp_post[l]), 1024, 512)
    return xf.reshape(b, s, d)
```

```python
import functools
import itertools
import math

import jax
import jax.numpy as jnp
from jax import lax
from jax.experimental import pallas as pl
from jax.experimental.pallas import tpu as pltpu

F32 = jnp.float32
BF16 = jnp.bfloat16

RMS_EPS = 1e-6
NEG_INF = -1e30
STAT_INIT = -(2.0 ** 100)

T5_BUCKETS = 32
T5_MAX_EXACT = T5_BUCKETS // 2
T5_MAX_DISTANCE = 1024

DIFF_HEADS = 4
DIFF_HEAD_DIM = 64
MOBA_HEADS = 4
MOBA_HEAD_DIM = 128
MOBA_BLOCK = 256
MOBA_TOPK = 3
MLA_HEADS = 8
MLA_Q_RANK = 256
MLA_KV_RANK = 128
MLA_NOPE_DIM = 64
MLA_ROPE_DIM = 32
MLA_V_DIM = 64
ROPE_THETA = 10000.0
N_BRANCH = 3
BRANCH_WIDTH = 512

LOG2E = math.log2(math.e)
MOBA_HEADS_PER_STEP = 2
MLA_HEADS_PER_STEP = 4
PIPELINE_UNROLL = 4
SUM_ROWS = 16
LANES = 128
ATTN_TILE = 512
VMEM_LIMIT = 48 * 1024 * 1024


def _cparams(*sem):
    return pltpu.CompilerParams(dimension_semantics=sem, vmem_limit_bytes=VMEM_LIMIT)


def _rms(x, w):
    return x * lax.rsqrt(jnp.mean(x * x, axis=-1, keepdims=True) + RMS_EPS) * w


def _dot(a, b):
    return jnp.dot(a, b, preferred_element_type=F32)


def _dot_nt(a, b):
    return lax.dot_general(a, b, (((1,), (1,)), ((), ())), preferred_element_type=F32)


def _norm_matmul_kernel(x_ref, nw_ref, w_ref, cs_ref, o_ref, h_ref):
    @pl.when(pl.program_id(1) == 0)
    def _():
        h_ref[...] = _rms(x_ref[...], nw_ref[...]).astype(BF16)

    o_ref[...] = (_dot(h_ref[...], w_ref[...]) * cs_ref[...]).astype(o_ref.dtype)


def _norm_matmul(x, nw, w, colscale, out_dtype, tm, tn):
    n, d = x.shape
    nout = w.shape[1]
    return pl.pallas_call(
        _norm_matmul_kernel,
        grid=(n // tm, nout // tn),
        in_specs=[
            pl.BlockSpec((tm, d), lambda i, j: (i, 0)),
            pl.BlockSpec((1, d), lambda i, j: (0, 0)),
            pl.BlockSpec((d, tn), lambda i, j: (0, j)),
            pl.BlockSpec((1, tn), lambda i, j: (0, j)),
        ],
        out_specs=pl.BlockSpec((tm, tn), lambda i, j: (i, j)),
        out_shape=jax.ShapeDtypeStruct((n, nout), out_dtype),
        scratch_shapes=[pltpu.VMEM((tm, d), BF16)],
        compiler_params=_cparams("parallel", "arbitrary"),
        name="norm_matmul",
    )(x, nw, w, colscale)


def _bias_tile_kernel(tab_ref, o_ref, *, tile):
    h = pl.program_id(0)
    d = pl.program_id(1)
    c = lax.broadcasted_iota(jnp.int32, (tile, tile), 0)
    r = lax.broadcasted_iota(jnp.int32, (tile, tile), 1)
    rel = d * tile + r - c
    n = jnp.maximum(rel, 0)
    nf = jnp.maximum(n, 1).astype(F32)
    large = T5_MAX_EXACT + (jnp.log(nf / T5_MAX_EXACT)
                            / math.log(T5_MAX_DISTANCE / T5_MAX_EXACT)
                            * (T5_BUCKETS - T5_MAX_EXACT)).astype(jnp.int32)
    large = jnp.minimum(large, T5_BUCKETS - 1)
    bucket = jnp.where(n < T5_MAX_EXACT, n, large)
    val = jnp.zeros((tile, tile), F32)
    for b in range(T5_BUCKETS):
        val = jnp.where(bucket == b, tab_ref[b, h], val)
    o_ref[0, 0] = jnp.where(rel >= 0, val * LOG2E, NEG_INF)


def _num_bias_tiles(tile):
    return -(-(T5_MAX_DISTANCE + tile - 1) // tile) + 1


def _bias_tiles(rel_bias, tile):
    nh = rel_bias.shape[1]
    nd = _num_bias_tiles(tile)
    return pl.pallas_call(
        functools.partial(_bias_tile_kernel, tile=tile),
        grid=(nh, nd),
        in_specs=[pl.BlockSpec(memory_space=pltpu.SMEM)],
        out_specs=pl.BlockSpec((1, 1, tile, tile), lambda h, d: (h, d, 0, 0)),
        out_shape=jax.ShapeDtypeStruct((nh, nd, tile, tile), F32),
        compiler_params=_cparams("parallel", "parallel"),
        name="t5_bias_tiles",
    )(rel_bias)


def _store_scores(s, s_ref, mx_ref):
    s_ref[...] = s.astype(BF16)
    mx_ref[...] = jnp.max(s, axis=0, keepdims=True).astype(BF16).astype(F32)


def _softmax_update(s_ref, mx_ref, vt, m_ref, acc_ref):
    m_prev = m_ref[...]
    m_new = jnp.maximum(m_prev, mx_ref[...])
    alpha = jnp.exp2(m_prev - m_new)
    p = jnp.exp2(s_ref[...] - m_new.astype(BF16))
    acc_ref[...] = alpha * acc_ref[...] + _dot(vt, p)
    m_ref[...] = m_new


def _paired_pipeline(i, nq, n_maps, fill, values, consume):
    def item(t):
        second = (jnp.int32(t) > i).astype(jnp.int32)
        j = t - second * (i + 1)
        dist = i + second * (nq - 1 - 2 * i) - j
        return second, j, dist

    def fill_item(t, slot):
        qt, j, dist = item(t)
        for mp in range(n_maps):
            fill(qt, j, dist, slot, mp)

    def step(t, slot):
        qt_f, j_f, dist_f = item(t + 1)
        qt_u, j_u, _ = item(t)
        vts = values(j_u)
        for mp in range(n_maps):
            fill(qt_f, j_f, dist_f, 1 - slot, mp)
            consume(qt_u, vts[mp], slot, mp)

    fill_item(0, 0)

    def trip(r, c):
        for u in range(PIPELINE_UNROLL):
            step(r * PIPELINE_UNROLL + u, u % 2)
        return c

    lax.fori_loop(0, nq // PIPELINE_UNROLL, trip, 0)
    qt, j, _ = item(nq)
    vts = values(j)
    for mp in range(n_maps):
        consume(qt, vts[mp], 0, mp)


def _values_t(v):
    return jnp.concatenate([v.T, jnp.ones((SUM_ROWS, v.shape[0]), v.dtype)], axis=0)


def _init_stats(m_ref, acc_ref):
    m_ref[...] = jnp.full(m_ref.shape, STAT_INIT, F32)
    acc_ref[...] = jnp.zeros(acc_ref.shape, F32)


def _stat_scratch(tile, dv):
    return [pltpu.VMEM((2, 1, tile), F32), pltpu.VMEM((2, dv + SUM_ROWS, tile), F32)]


def _score_scratch(n_maps, tile):
    return [pltpu.VMEM((2, n_maps, tile, tile), BF16), pltpu.VMEM((2, n_maps, 1, tile), F32)]


def _normalized(acc_ref, qt, dv):
    return acc_ref[qt, :dv, :] / acc_ref[qt, dv:dv + 1, :]


def _paired_query_specs(tile, width, nq, col_of):
    return [pl.BlockSpec((1, tile, width), lambda bi, h, i: (bi, i, col_of(h))),
            pl.BlockSpec((1, tile, width), lambda bi, h, i: (bi, nq - 1 - i, col_of(h)))]


def _paired_outputs(b, s, tile, width, total_width):
    half = s // tile // 2
    specs = [pl.BlockSpec((1, tile, width), lambda bi, h, i: (bi, i, h)),
             pl.BlockSpec((1, tile, width), lambda bi, h, i: (bi, half - 1 - i, h))]
    shapes = [jax.ShapeDtypeStruct((b, s // 2, total_width), BF16)] * 2
    return specs, shapes


def _key_rows(j, tile):
    return pl.ds(pl.multiple_of(j * tile, tile), tile)


def _diff_kernel(scal_ref, lam_ref, subw_ref, qa_ref, qb_ref, k_ref, v_ref, bias_ref, oa_ref, ob_ref,
                 qm_ref, s_ref, mx_ref, m1, a1, m2, a2, *, tile, nd, nq):
    i = pl.program_id(2)
    for qt, q_ref in enumerate((qa_ref, qb_ref)):
        q = q_ref[0]
        lane = lax.broadcasted_iota(jnp.int32, q.shape, 1)
        zero = jnp.zeros_like(q)
        qm_ref[qt, 0] = jnp.where(lane < DIFF_HEAD_DIM, q, zero)
        qm_ref[qt, 1] = jnp.where(lane >= DIFF_HEAD_DIM, q, zero)
    stats = ((m1, a1), (m2, a2))
    for st in stats:
        _init_stats(*st)

    def fill(qt, j, dist, slot, mp):
        k = k_ref[0, _key_rows(j, tile), :]
        bias = bias_ref[0, jnp.minimum(dist, nd - 1)]
        _store_scores(_dot_nt(k, qm_ref[qt, mp]) + bias, s_ref.at[slot, mp], mx_ref.at[slot, mp])

    def values(j):
        return [_values_t(v_ref[0, _key_rows(j, tile), :])] * 2

    def consume(qt, vt, slot, mp):
        m_ref, acc_ref = stats[mp]
        _softmax_update(s_ref.at[slot, mp], mx_ref.at[slot, mp], vt, m_ref.at[qt], acc_ref.at[qt])

    _paired_pipeline(i, nq, 2, fill, values, consume)

    lam_init = scal_ref[0]
    lv = lam_ref[...]
    lam = (jnp.exp(jnp.sum(lv[0:1] * lv[1:2], axis=1, keepdims=True))
           - jnp.exp(jnp.sum(lv[2:3] * lv[3:4], axis=1, keepdims=True)) + lam_init)
    hd = 2 * DIFF_HEAD_DIM
    for qt, o_ref in enumerate((oa_ref, ob_ref)):
        o = (_normalized(a1, qt, hd) - lam * _normalized(a2, qt, hd)).T
        o_ref[0] = (_rms(o, subw_ref[...]) * (1.0 - lam_init)).astype(o_ref.dtype)


def _diff_attention(qkv, bias, scal, lam_vecs, subw, tile):
    b, s, _ = qkv.shape
    nd = bias.shape[1]
    nq = s // tile
    hd = 2 * DIFF_HEAD_DIM
    out_specs, out_shapes = _paired_outputs(b, s, tile, hd, DIFF_HEADS * hd)
    lo, hi = pl.pallas_call(
        functools.partial(_diff_kernel, tile=tile, nd=nd, nq=nq),
        grid=(b, DIFF_HEADS, nq // 2),
        in_specs=[
            pl.BlockSpec(memory_space=pltpu.SMEM),
            pl.BlockSpec(lam_vecs.shape, lambda bi, h, i: (0, 0)),
            pl.BlockSpec((1, hd), lambda bi, h, i: (0, 0)),
            *_paired_query_specs(tile, hd, nq, lambda h: h),
            pl.BlockSpec((1, s, hd), lambda bi, h, i: (bi, 0, DIFF_HEADS + h)),
            pl.BlockSpec((1, s, hd), lambda bi, h, i: (bi, 0, 2 * DIFF_HEADS + h)),
            pl.BlockSpec((1, nd, tile, tile), lambda bi, h, i: (h, 0, 0, 0)),
        ],
        out_specs=out_specs,
        out_shape=out_shapes,
        scratch_shapes=[pltpu.VMEM((2, 2, tile, hd), BF16)] + _score_scratch(2, tile)
        + _stat_scratch(tile, hd) + _stat_scratch(tile, hd),
        compiler_params=_cparams("parallel", "parallel", "arbitrary"),
        name="diff_attention",
    )(scal, lam_vecs, subw, qkv, qkv, qkv, qkv, bias)
    return jnp.concatenate([lo, hi], axis=1)


def _moba_kernel(qa_ref, qb_ref, k_ref, oh_ref, v_ref, bias_ref, oa_ref, ob_ref,
                 kmean_ref, qaug_ref, s_ref, mx_ref, *stat_refs, tile, nd, nblk, heads, nq):
    blk = MOBA_BLOCK
    hd = MOBA_HEAD_DIM
    per_tile = tile // blk
    i = pl.program_id(2)
    stats = [stat_refs[2 * hh:2 * hh + 2] for hh in range(heads)]

    @pl.when(i == 0)
    def _():
        for hh in range(heads):
            for n in range(nblk):
                kb = k_ref[0, n * blk:(n + 1) * blk, hh * hd:(hh + 1) * hd].astype(F32)
                mean = jnp.mean(kb, axis=0, keepdims=True)
                for part in range(3):
                    term = mean.astype(BF16).astype(F32)
                    kmean_ref[hh, part * nblk + n:part * nblk + n + 1, :] = term
                    mean = mean - term

    for hh, (qt, q_ref) in itertools.product(range(heads), enumerate((qa_ref, qb_ref))):
        q = q_ref[0, :, hh * hd:(hh + 1) * hd]
        q_tile = i if qt == 0 else nq - 1 - i
        parts = _dot_nt(kmean_ref[hh].astype(BF16), q)
        gate = parts[:nblk] + parts[nblk:2 * nblk] + parts[2 * nblk:]
        blk_id = lax.broadcasted_iota(jnp.int32, gate.shape, 0)
        blk_f = blk_id.astype(F32)
        lane = lax.broadcasted_iota(jnp.int32, gate.shape, 1)
        own = q_tile * per_tile + lax.shift_right_logical(lane, int(math.log2(blk)))
        valid = blk_id < own
        gate = jnp.where(valid, gate, NEG_INF)
        sel = blk_id == own
        for _ in range(MOBA_TOPK):
            top = jnp.max(gate, axis=0, keepdims=True)
            first = jnp.min(jnp.where(gate == top, blk_f, float(nblk)), axis=0, keepdims=True)
            hit = blk_f == first
            sel = jnp.logical_or(sel, jnp.logical_and(hit, valid))
            gate = jnp.where(hit, -3.0e38, gate)
        pen = jnp.where(sel, 0.0, NEG_INF)
        pen = jnp.concatenate([pen, jnp.zeros((LANES - nblk, tile), F32)], axis=0)
        qaug_ref[qt, hh, :, :LANES] = q
        qaug_ref[qt, hh, :, LANES:] = pen.T.astype(BF16)
    for st in stats:
        _init_stats(*st)

    def fill(qt, j, dist, slot, hh):
        rows = _key_rows(j, tile)
        kaug = jnp.concatenate([k_ref[0, rows, hh * hd:(hh + 1) * hd], oh_ref[rows, :]], axis=1)
        bias = bias_ref[hh, jnp.minimum(dist, nd - 1)]
        _store_scores(_dot_nt(kaug, qaug_ref[qt, hh]) + bias, s_ref.at[slot, hh], mx_ref.at[slot, hh])

    def values(j):
        rows = _key_rows(j, tile)
        return [_values_t(v_ref[0, rows, hh * hd:(hh + 1) * hd]) for hh in range(heads)]

    def consume(qt, vt, slot, hh):
        m_ref, acc_ref = stats[hh]
        _softmax_update(s_ref.at[slot, hh], mx_ref.at[slot, hh], vt, m_ref.at[qt], acc_ref.at[qt])

    _paired_pipeline(i, nq, heads, fill, values, consume)
    for hh, (qt, o_ref) in itertools.product(range(heads), enumerate((oa_ref, ob_ref))):
        o_ref[0, :, hh * hd:(hh + 1) * hd] = _normalized(stats[hh][1], qt, hd).T.astype(o_ref.dtype)


def _moba_attention(qkv, bias, tile):
    b, s, _ = qkv.shape
    nd = bias.shape[1]
    nblk = s // MOBA_BLOCK
    hd = MOBA_HEAD_DIM
    hg = MOBA_HEADS_PER_STEP
    w = hg * hd
    col0 = 3 * DIFF_HEADS // hg
    groups = MOBA_HEADS // hg
    onehot = (jnp.arange(s)[:, None] // MOBA_BLOCK == jnp.arange(LANES)[None, :]).astype(BF16)
    nq = s // tile
    out_specs, out_shapes = _paired_outputs(b, s, tile, w, MOBA_HEADS * hd)
    lo, hi = pl.pallas_call(
        functools.partial(_moba_kernel, tile=tile, nd=nd, nblk=nblk, heads=hg, nq=nq),
        grid=(b, groups, nq // 2),
        in_specs=[
            *_paired_query_specs(tile, w, nq, lambda h: col0 + h),
            pl.BlockSpec((1, s, w), lambda bi, h, i: (bi, 0, col0 + groups + h)),
            pl.BlockSpec((s, LANES), lambda bi, h, i: (0, 0)),
            pl.BlockSpec((1, s, w), lambda bi, h, i: (bi, 0, col0 + 2 * groups + h)),
            pl.BlockSpec((hg, nd, tile, tile), lambda bi, h, i: (DIFF_HEADS // hg + h, 0, 0, 0)),
        ],
        out_specs=out_specs,
        out_shape=out_shapes,
        scratch_shapes=[pltpu.VMEM((hg, 3 * nblk, hd), F32), pltpu.VMEM((2, hg, tile, 2 * LANES), BF16)]
        + _score_scratch(hg, tile) + hg * _stat_scratch(tile, hd),
        compiler_params=_cparams("parallel", "parallel", "arbitrary"),
        name="moba_attention",
    )(qkv, qkv, qkv, onehot, qkv, bias)
    return jnp.concatenate([lo, hi], axis=1)


def _mla_prep_kernel(small_ref, qnw_ref, kvnw_ref, wuq_ref, wukv_ref, ct_ref, st_ref,
                     q_ref, k_ref, v_ref, *, scale):
    kw = MLA_HEADS * LANES
    ct = ct_ref[...]
    st = st_ref[...]
    cq = small_ref[:, :MLA_Q_RANK]
    ckv = small_ref[:, MLA_Q_RANK:MLA_Q_RANK + MLA_KV_RANK]
    kpe = small_ref[:, MLA_Q_RANK + MLA_KV_RANK:MLA_Q_RANK + MLA_KV_RANK + LANES]
    kpe_rot = small_ref[:, MLA_Q_RANK + MLA_KV_RANK + LANES:]
    qq = _dot(_rms(cq, qnw_ref[...]).astype(BF16), wuq_ref[...])
    kk = _dot(_rms(ckv, kvnw_ref[...]).astype(BF16), wukv_ref[...])
    k_rope = kpe * ct + kpe_rot * st
    for h in range(MLA_HEADS):
        cols = slice(h * LANES, (h + 1) * LANES)
        rot = qq[:, kw + h * LANES:kw + (h + 1) * LANES]
        q_ref[:, cols] = ((qq[:, cols] * ct + rot * st) * scale).astype(BF16)
        k_ref[:, cols] = (kk[:, cols] + k_rope).astype(BF16)
    v_ref[...] = kk[:, kw:].astype(BF16)


def _mla_prep(small, qnw, kvnw, wuq, wukv, ct, st, tm):
    n = small.shape[0]
    kw = MLA_HEADS * LANES
    vw = MLA_HEADS * MLA_V_DIM
    scale = LOG2E / math.sqrt(MLA_NOPE_DIM + MLA_ROPE_DIM)
    row = lambda i: (i, 0)
    const = lambda i: (0, 0)
    return pl.pallas_call(
        functools.partial(_mla_prep_kernel, scale=scale),
        grid=(n // tm,),
        in_specs=[
            pl.BlockSpec((tm, small.shape[1]), row),
            pl.BlockSpec(qnw.shape, const),
            pl.BlockSpec(kvnw.shape, const),
            pl.BlockSpec(wuq.shape, const),
            pl.BlockSpec(wukv.shape, const),
            pl.BlockSpec((tm, LANES), row),
            pl.BlockSpec((tm, LANES), row),
        ],
        out_specs=[pl.BlockSpec((tm, kw), row), pl.BlockSpec((tm, kw), row), pl.BlockSpec((tm, vw), row)],
        out_shape=[jax.ShapeDtypeStruct((n, kw), BF16), jax.ShapeDtypeStruct((n, kw), BF16),
                   jax.ShapeDtypeStruct((n, vw), BF16)],
        compiler_params=_cparams("parallel"),
        name="mla_prep",
    )(small, qnw, kvnw, wuq, wukv, ct, st)


def _mla_kernel(qa_ref, qb_ref, k_ref, v_ref, oa_ref, ob_ref, q_ref, mask_ref, s_ref, mx_ref, *stat_refs,
                tile, heads, nq):
    i = pl.program_id(2)
    stats = [stat_refs[2 * hh:2 * hh + 2] for hh in range(heads)]
    for st in stats:
        _init_stats(*st)
    q_ref[0] = qa_ref[0]
    q_ref[1] = qb_ref[0]
    key = lax.broadcasted_iota(jnp.int32, (tile, tile), 0)
    qry = lax.broadcasted_iota(jnp.int32, (tile, tile), 1)
    mask_ref[0] = jnp.where(qry >= key, 0.0, NEG_INF)
    mask_ref[1] = jnp.zeros((tile, tile), F32)

    def fill(qt, j, dist, slot, hh):
        rows = _key_rows(j, tile)
        head = slice(hh * LANES, (hh + 1) * LANES)
        mask = mask_ref[jnp.minimum(dist, 1)]
        _store_scores(_dot_nt(k_ref[0, rows, head], q_ref[qt, :, head]) + mask,
                      s_ref.at[slot, hh], mx_ref.at[slot, hh])

    def values(j):
        rows = _key_rows(j, tile)
        ones = jnp.ones((SUM_ROWS, tile), BF16)
        out = []
        for pr in range(heads // 2):
            vt = v_ref[0, rows, pr * LANES:(pr + 1) * LANES].T
            out += [jnp.concatenate([vt[:MLA_V_DIM], ones], axis=0), jnp.concatenate([vt[MLA_V_DIM:], ones], axis=0)]
        return out

    def consume(qt, vt, slot, hh):
        m_ref, acc_ref = stats[hh]
        _softmax_update(s_ref.at[slot, hh], mx_ref.at[slot, hh], vt, m_ref.at[qt], acc_ref.at[qt])

    _paired_pipeline(i, nq, heads, fill, values, consume)
    for pr, (qt, o_ref) in itertools.product(range(heads // 2), enumerate((oa_ref, ob_ref))):
        o = jnp.concatenate([_normalized(stats[2 * pr][1], qt, MLA_V_DIM),
                             _normalized(stats[2 * pr + 1][1], qt, MLA_V_DIM)], axis=0)
        o_ref[0, :, pr * LANES:(pr + 1) * LANES] = o.T.astype(o_ref.dtype)


def _mla_attention(q, k, v, tile):
    b, s, _ = q.shape
    hg = MLA_HEADS_PER_STEP
    groups = MLA_HEADS // hg
    nq = s // tile
    out_specs, out_shapes = _paired_outputs(b, s, tile, hg * MLA_V_DIM, MLA_HEADS * MLA_V_DIM)
    lo, hi = pl.pallas_call(
        functools.partial(_mla_kernel, tile=tile, heads=hg, nq=nq),
        grid=(b, groups, nq // 2),
        in_specs=[
            *_paired_query_specs(tile, hg * LANES, nq, lambda h: h),
            pl.BlockSpec((1, s, hg * LANES), lambda bi, h, i: (bi, 0, h)),
            pl.BlockSpec((1, s, hg * MLA_V_DIM), lambda bi, h, i: (bi, 0, h)),
        ],
        out_specs=out_specs,
        out_shape=out_shapes,
        scratch_shapes=[pltpu.VMEM((2, tile, hg * LANES), BF16), pltpu.VMEM((2, tile, tile), F32)]
        + _score_scratch(hg, tile) + hg * _stat_scratch(tile, MLA_V_DIM),
        compiler_params=_cparams("parallel", "parallel", "arbitrary"),
        name="mla_attention",
    )(q, q, k, v)
    return jnp.concatenate([lo, hi], axis=1)


def _merge_kernel(x_ref, oa_ref, ob_ref, oc_ref, g_ref, wb_ref, wo_ref, nw_ref, o_ref):
    d = x_ref.shape[1]
    mixed = jnp.zeros(x_ref.shape, F32)
    for gi, br_ref in enumerate((oa_ref, ob_ref, oc_ref)):
        br = _dot(br_ref[...], wb_ref[gi])
        gate = 1.0 / (1.0 + jnp.exp(-g_ref[:, gi * d:(gi + 1) * d].astype(F32)))
        mixed = mixed + gate * br
    y = _dot(mixed.astype(BF16), wo_ref[...])
    o_ref[...] = x_ref[...] + _rms(y, nw_ref[...])


def _merge(x, oa, ob, oc, g, wb, wo, nw, tm):
    n, d = x.shape
    row = lambda i: (i, 0)
    return pl.pallas_call(
        _merge_kernel,
        grid=(n // tm,),
        in_specs=[
            pl.BlockSpec((tm, d), row),
            pl.BlockSpec((tm, BRANCH_WIDTH), row),
            pl.BlockSpec((tm, BRANCH_WIDTH), row),
            pl.BlockSpec((tm, BRANCH_WIDTH), row),
            pl.BlockSpec((tm, N_BRANCH * d), row),
            pl.BlockSpec(wb.shape, lambda i: (0, 0, 0)),
            pl.BlockSpec(wo.shape, lambda i: (0, 0)),
            pl.BlockSpec((1, d), lambda i: (0, 0)),
        ],
        out_specs=pl.BlockSpec((tm, d), row),
        out_shape=jax.ShapeDtypeStruct((n, d), F32),
        compiler_params=_cparams("parallel"),
        name="branch_merge",
    )(x, oa, ob, oc, g, wb, wo, nw)


def _mlp_kernel(x_ref, nw1_ref, wu_ref, wd_ref, nw2_ref, o_ref, h_ref, acc_ref):
    j = pl.program_id(1)

    @pl.when(j == 0)
    def _():
        h_ref[...] = _rms(x_ref[...], nw1_ref[...]).astype(BF16)
        acc_ref[...] = jnp.zeros(acc_ref.shape, F32)

    u = jnp.square(jnp.maximum(_dot(h_ref[...], wu_ref[...]), 0.0))
    acc_ref[...] += _dot(u.astype(BF16), wd_ref[...])

    @pl.when(j == pl.num_programs(1) - 1)
    def _():
        o_ref[...] = x_ref[...] + _rms(acc_ref[...], nw2_ref[...])


def _mlp(x, nw1, wu, wd, nw2, tm, tf):
    n, d = x.shape
    f = wu.shape[1]
    return pl.pallas_call(
        _mlp_kernel,
        grid=(n // tm, f // tf),
        in_specs=[
            pl.BlockSpec((tm, d), lambda i, j: (i, 0)),
            pl.BlockSpec((1, d), lambda i, j: (0, 0)),
            pl.BlockSpec((d, tf), lambda i, j: (0, j)),
            pl.BlockSpec((tf, d), lambda i, j: (j, 0)),
            pl.BlockSpec((1, d), lambda i, j: (0, 0)),
        ],
        out_specs=pl.BlockSpec((tm, d), lambda i, j: (i, 0)),
        out_shape=jax.ShapeDtypeStruct((n, d), F32),
        scratch_shapes=[pltpu.VMEM((tm, d), BF16), pltpu.VMEM((tm, d), F32)],
        compiler_params=_cparams("parallel", "arbitrary"),
        name="relu2_mlp",
    )(x, nw1, wu, wd, nw2)


def _rotate_half_cols(w):
    half = w.shape[-1] // 2
    return jnp.concatenate([-w[..., half:], w[..., :half]], axis=-1)


def _pad_cols(w, before, total):
    pad = [(0, 0)] * (w.ndim - 1) + [(before, total - before - w.shape[-1])]
    return jnp.pad(w, pad)


def _layer_weights(w_in, w_uq, w_ukv):
    n_qkv = 3 * DIFF_HEADS * 2 * DIFF_HEAD_DIM + 3 * MOBA_HEADS * MOBA_HEAD_DIM
    o_cq = n_qkv
    o_ckv = o_cq + MLA_Q_RANK
    o_kpe = o_ckv + MLA_KV_RANK
    o_g = o_kpe + MLA_ROPE_DIM
    w_qkv = w_in[:, :n_qkv].astype(BF16)
    w_kpe = w_in[:, o_kpe:o_g]
    w_small = jnp.concatenate([
        w_in[:, o_cq:o_kpe],
        _pad_cols(w_kpe, MLA_NOPE_DIM, LANES),
        _pad_cols(_rotate_half_cols(w_kpe), MLA_NOPE_DIM, LANES)], axis=1).astype(BF16)
    w_g = w_in[:, o_g:].astype(BF16)

    qk_dim = MLA_NOPE_DIM + MLA_ROPE_DIM
    uq = w_uq.reshape(MLA_Q_RANK, MLA_HEADS, qk_dim)
    uq_plain = _pad_cols(uq, 0, LANES).reshape(MLA_Q_RANK, MLA_HEADS * LANES)
    uq_rot = _pad_cols(_rotate_half_cols(uq[..., MLA_NOPE_DIM:]), MLA_NOPE_DIM, LANES)
    wuq = jnp.concatenate([uq_plain, uq_rot.reshape(MLA_Q_RANK, MLA_HEADS * LANES)], axis=1).astype(BF16)

    ukv = w_ukv.reshape(MLA_KV_RANK, MLA_HEADS, MLA_NOPE_DIM + MLA_V_DIM)
    uk = _pad_cols(ukv[..., :MLA_NOPE_DIM], 0, LANES).reshape(MLA_KV_RANK, MLA_HEADS * LANES)
    uv = ukv[..., MLA_NOPE_DIM:].reshape(MLA_KV_RANK, MLA_HEADS * MLA_V_DIM)
    wukv = jnp.concatenate([uk, uv], axis=1).astype(BF16)
    return w_qkv, w_small, w_g, wuq, wukv


def kernel(x, positions, rel_bias, norm_mix_pre, norm_mix_post, norm_mlp_pre, norm_mlp_post, w_in, diff_lambda, diff_subln, mla_q_norm, mla_w_uq, mla_kv_norm, mla_w_ukv, w_branch, w_out, w_up, w_down):
    b, s, d = x.shape
    n = b * s
    depth = w_in.shape[0]
    tile = ATTN_TILE
    assert tile % MOBA_BLOCK == 0 and n % 2048 == 0 and (s // MOBA_BLOCK) % 8 == 0
    assert s % (tile * PIPELINE_UNROLL) == 0 and PIPELINE_UNROLL % 2 == 0

    xf = x.reshape(n, d)
    bias = _bias_tiles(rel_bias, tile)

    half = MLA_ROPE_DIM // 2
    inv_freq = ROPE_THETA ** (-jnp.arange(half, dtype=F32) * 2.0 / MLA_ROPE_DIM)
    ang = positions.astype(F32).reshape(n, 1) * inv_freq
    cos, sin = jnp.cos(ang), jnp.sin(ang)
    ct = jnp.concatenate([jnp.ones((n, MLA_NOPE_DIM), F32), cos, cos,
                          jnp.zeros((n, LANES - MLA_NOPE_DIM - MLA_ROPE_DIM), F32)], axis=1)
    st = jnp.concatenate([jnp.zeros((n, MLA_NOPE_DIM), F32), sin, sin,
                          jnp.zeros((n, LANES - MLA_NOPE_DIM - MLA_ROPE_DIM), F32)], axis=1)

    n_diff = DIFF_HEADS * 2 * DIFF_HEAD_DIM
    n_moba = MOBA_HEADS * MOBA_HEAD_DIM
    qkv_scale = jnp.concatenate([
        jnp.full((n_diff,), LOG2E / math.sqrt(DIFF_HEAD_DIM), F32), jnp.ones((2 * n_diff,), F32),
        jnp.full((n_moba,), LOG2E / math.sqrt(MOBA_HEAD_DIM), F32), jnp.ones((2 * n_moba,), F32)]).reshape(1, -1)

    row = lambda a: a.reshape(1, -1)
    for l in range(depth):
        lam_init = 0.8 - 0.6 * math.exp(-0.3 * l)
        w_qkv, w_small, w_g, wuq, wukv = _layer_weights(w_in[l], mla_w_uq[l], mla_w_ukv[l])
        nw = row(norm_mix_pre[l])
        qkv = _norm_matmul(xf, nw, w_qkv, qkv_scale, BF16, 2048, 1024)
        small = _norm_matmul(xf, nw, w_small, jnp.ones((1, w_small.shape[1]), F32), F32, 2048, w_small.shape[1])
        g = _norm_matmul(xf, nw, w_g, jnp.ones((1, w_g.shape[1]), F32), BF16, 2048, 1024)

        qkv3 = qkv.reshape(b, s, -1)
        scal = jnp.full((1,), lam_init, F32)
        oa = _diff_attention(qkv3, bias, scal, diff_lambda[l], row(diff_subln[l]), tile)
        ob = _moba_attention(qkv3, bias, tile)
        mq, mk, mv = _mla_prep(small, row(mla_q_norm[l]), row(mla_kv_norm[l]), wuq, wukv, ct, st, 512)
        oc = _mla_attention(mq.reshape(b, s, -1), mk.reshape(b, s, -1), mv.reshape(b, s, -1), tile)

        xf = _merge(xf, oa.reshape(n, -1), ob.reshape(n, -1), oc.reshape(n, -1), g,
                    w_branch[l].astype(BF16), w_out[l].astype(BF16), row(norm_mix_post[l]), 512)
        xf = _mlp(xf, row(norm_mlp_pre[l]), w_up[l].astype(BF16), w_down[l].astype(BF16),
                  row(norm_mlp_post[l]), 1024, 1024)
    return xf.reshape(b, s, d)
```

```python
import functools
import itertools
import math

import jax
import jax.numpy as jnp
from jax import lax
from jax.experimental import pallas as pl
from jax.experimental.pallas import tpu as pltpu

F32 = jnp.float32
BF16 = jnp.bfloat16

RMS_EPS = 1e-6
NEG_INF = -1e30
STAT_INIT = -(2.0 ** 100)

T5_BUCKETS = 32
T5_MAX_EXACT = T5_BUCKETS // 2
T5_MAX_DISTANCE = 1024

DIFF_HEADS = 4
DIFF_HEAD_DIM = 64
MOBA_HEADS = 4
MOBA_HEAD_DIM = 128
MOBA_BLOCK = 256
MOBA_TOPK = 3
MLA_HEADS = 8
MLA_Q_RANK = 256
MLA_KV_RANK = 128
MLA_NOPE_DIM = 64
MLA_ROPE_DIM = 32
MLA_V_DIM = 64
ROPE_THETA = 10000.0
N_BRANCH = 3
BRANCH_WIDTH = 512

LOG2E = math.log2(math.e)
MOBA_HEADS_PER_STEP = 2
MLA_HEADS_PER_STEP = 4
PIPELINE_UNROLL = 4
SUM_ROWS = 16
LANES = 128
ATTN_TILE = 512
VMEM_LIMIT = 48 * 1024 * 1024


def _cparams(*sem):
    return pltpu.CompilerParams(dimension_semantics=sem, vmem_limit_bytes=VMEM_LIMIT)


def _rms(x, w):
    return x * lax.rsqrt(jnp.mean(x * x, axis=-1, keepdims=True) + RMS_EPS) * w


def _dot(a, b):
    return jnp.dot(a, b, preferred_element_type=F32)


def _dot_nt(a, b):
    return lax.dot_general(a, b, (((1,), (1,)), ((), ())), preferred_element_type=F32)


def _norm_matmul_kernel(x_ref, nw_ref, w_ref, cs_ref, o_ref, h_ref):
    @pl.when(pl.program_id(1) == 0)
    def _():
        h_ref[...] = _rms(x_ref[...], nw_ref[...]).astype(BF16)

    o_ref[...] = (_dot(h_ref[...], w_ref[...]) * cs_ref[...]).astype(o_ref.dtype)


def _norm_matmul(x, nw, w, colscale, out_dtype, tm, tn):
    n, d = x.shape
    nout = w.shape[1]
    return pl.pallas_call(
        _norm_matmul_kernel,
        grid=(n // tm, nout // tn),
        in_specs=[
            pl.BlockSpec((tm, d), lambda i, j: (i, 0)),
            pl.BlockSpec((1, d), lambda i, j: (0, 0)),
            pl.BlockSpec((d, tn), lambda i, j: (0, j)),
            pl.BlockSpec((1, tn), lambda i, j: (0, j)),
        ],
        out_specs=pl.BlockSpec((tm, tn), lambda i, j: (i, j)),
        out_shape=jax.ShapeDtypeStruct((n, nout), out_dtype),
        scratch_shapes=[pltpu.VMEM((tm, d), BF16)],
        compiler_params=_cparams("parallel", "arbitrary"),
        name="norm_matmul",
    )(x, nw, w, colscale)


def _bias_tile_kernel(tab_ref, o_ref, *, tile):
    h = pl.program_id(0)
    d = pl.program_id(1)
    c = lax.broadcasted_iota(jnp.int32, (tile, tile), 0)
    r = lax.broadcasted_iota(jnp.int32, (tile, tile), 1)
    rel = d * tile + r - c
    n = jnp.maximum(rel, 0)
    nf = jnp.maximum(n, 1).astype(F32)
    large = T5_MAX_EXACT + (jnp.log(nf / T5_MAX_EXACT)
                            / math.log(T5_MAX_DISTANCE / T5_MAX_EXACT)
                            * (T5_BUCKETS - T5_MAX_EXACT)).astype(jnp.int32)
    large = jnp.minimum(large, T5_BUCKETS - 1)
    bucket = jnp.where(n < T5_MAX_EXACT, n, large)
    val = jnp.zeros((tile, tile), F32)
    for b in range(T5_BUCKETS):
        val = jnp.where(bucket == b, tab_ref[b, h], val)
    o_ref[0, 0] = jnp.where(rel >= 0, val * LOG2E, NEG_INF)


def _num_bias_tiles(tile):
    return -(-(T5_MAX_DISTANCE + tile - 1) // tile) + 1


def _bias_tiles(rel_bias, tile):
    nh = rel_bias.shape[1]
    nd = _num_bias_tiles(tile)
    return pl.pallas_call(
        functools.partial(_bias_tile_kernel, tile=tile),
        grid=(nh, nd),
        in_specs=[pl.BlockSpec(memory_space=pltpu.SMEM)],
        out_specs=pl.BlockSpec((1, 1, tile, tile), lambda h, d: (h, d, 0, 0)),
        out_shape=jax.ShapeDtypeStruct((nh, nd, tile, tile), F32),
        compiler_params=_cparams("parallel", "parallel"),
        name="t5_bias_tiles",
    )(rel_bias)


def _store_scores(s, s_ref, mx_ref):
    s_ref[...] = s.astype(BF16)
    mx_ref[...] = jnp.max(s, axis=0, keepdims=True).astype(BF16).astype(F32)


def _softmax_update(s_ref, mx_ref, vt, m_ref, acc_ref):
    m_prev = m_ref[...]
    m_new = jnp.maximum(m_prev, mx_ref[...])
    alpha = jnp.exp2(m_prev - m_new)
    p = jnp.exp2(s_ref[...] - m_new.astype(BF16))
    acc_ref[...] = alpha * acc_ref[...] + _dot(vt, p)
    m_ref[...] = m_new


def _paired_pipeline(i, nq, n_maps, fill, values, consume):
    def item(t):
        second = (jnp.int32(t) > i).astype(jnp.int32)
        j = t - second * (i + 1)
        dist = i + second * (nq - 1 - 2 * i) - j
        return second, j, dist

    def fill_item(t, slot):
        qt, j, dist = item(t)
        for mp in range(n_maps):
            fill(qt, j, dist, slot, mp)

    def step(t, slot):
        qt_f, j_f, dist_f = item(t + 1)
        qt_u, j_u, _ = item(t)
        vts = values(j_u)
        for mp in range(n_maps):
            fill(qt_f, j_f, dist_f, 1 - slot, mp)
            consume(qt_u, vts[mp], slot, mp)

    fill_item(0, 0)

    def trip(r, c):
        for u in range(PIPELINE_UNROLL):
            step(r * PIPELINE_UNROLL + u, u % 2)
        return c

    lax.fori_loop(0, nq // PIPELINE_UNROLL, trip, 0)
    qt, j, _ = item(nq)
    vts = values(j)
    for mp in range(n_maps):
        consume(qt, vts[mp], 0, mp)


def _values_t(v):
    return jnp.concatenate([v.T, jnp.ones((SUM_ROWS, v.shape[0]), v.dtype)], axis=0)


def _init_stats(m_ref, acc_ref):
    m_ref[...] = jnp.full(m_ref.shape, STAT_INIT, F32)
    acc_ref[...] = jnp.zeros(acc_ref.shape, F32)


def _stat_scratch(tile, dv):
    return [pltpu.VMEM((2, 1, tile), F32), pltpu.VMEM((2, dv + SUM_ROWS, tile), F32)]


def _score_scratch(n_maps, tile):
    return [pltpu.VMEM((2, n_maps, tile, tile), BF16), pltpu.VMEM((2, n_maps, 1, tile), F32)]


def _normalized(acc_ref, qt, dv):
    return acc_ref[qt, :dv, :] / acc_ref[qt, dv:dv + 1, :]


def _paired_query_specs(tile, width, nq, col_of):
    return [pl.BlockSpec((1, tile, width), lambda bi, h, i: (bi, i, col_of(h))),
            pl.BlockSpec((1, tile, width), lambda bi, h, i: (bi, nq - 1 - i, col_of(h)))]


def _paired_output(b, s, tile, width, total_width):
    spec = pl.BlockSpec((1, 1, 2, tile, width), lambda bi, h, i: (bi, i, 0, 0, h))
    return spec, jax.ShapeDtypeStruct((b, s // tile // 2, 2, tile, total_width), BF16)


def _pair_order_rows(nq, tile, width):
    def index(r):
        t = lax.rem(r, nq)
        return lax.div(r, nq), jnp.minimum(t, nq - 1 - t), (t >= nq // 2).astype(jnp.int32), 0, 0
    return pl.BlockSpec((1, 1, 1, tile, width), index)


def _key_rows(j, tile):
    return pl.ds(pl.multiple_of(j * tile, tile), tile)


def _diff_kernel(scal_ref, lam_ref, subw_ref, qa_ref, qb_ref, k_ref, v_ref, bias_ref, o_ref,
                 qm_ref, s_ref, mx_ref, m1, a1, m2, a2, *, tile, nd, nq):
    i = pl.program_id(2)
    for qt, q_ref in enumerate((qa_ref, qb_ref)):
        q = q_ref[0]
        lane = lax.broadcasted_iota(jnp.int32, q.shape, 1)
        zero = jnp.zeros_like(q)
        qm_ref[qt, 0] = jnp.where(lane < DIFF_HEAD_DIM, q, zero)
        qm_ref[qt, 1] = jnp.where(lane >= DIFF_HEAD_DIM, q, zero)
    stats = ((m1, a1), (m2, a2))
    for st in stats:
        _init_stats(*st)

    def fill(qt, j, dist, slot, mp):
        k = k_ref[0, _key_rows(j, tile), :]
        bias = bias_ref[0, jnp.minimum(dist, nd - 1)]
        _store_scores(_dot_nt(k, qm_ref[qt, mp]) + bias, s_ref.at[slot, mp], mx_ref.at[slot, mp])

    def values(j):
        return [_values_t(v_ref[0, _key_rows(j, tile), :])] * 2

    def consume(qt, vt, slot, mp):
        m_ref, acc_ref = stats[mp]
        _softmax_update(s_ref.at[slot, mp], mx_ref.at[slot, mp], vt, m_ref.at[qt], acc_ref.at[qt])

    _paired_pipeline(i, nq, 2, fill, values, consume)

    lam_init = scal_ref[0]
    lv = lam_ref[...]
    lam = (jnp.exp(jnp.sum(lv[0:1] * lv[1:2], axis=1, keepdims=True))
           - jnp.exp(jnp.sum(lv[2:3] * lv[3:4], axis=1, keepdims=True)) + lam_init)
    hd = 2 * DIFF_HEAD_DIM
    for qt in range(2):
        o = (_normalized(a1, qt, hd) - lam * _normalized(a2, qt, hd)).T
        o_ref[0, 0, qt] = (_rms(o, subw_ref[...]) * (1.0 - lam_init)).astype(o_ref.dtype)


def _diff_attention(qkv, bias, scal, lam_vecs, subw, tile):
    b, s, _ = qkv.shape
    nd = bias.shape[1]
    nq = s // tile
    hd = 2 * DIFF_HEAD_DIM
    out_spec, out_shape = _paired_output(b, s, tile, hd, DIFF_HEADS * hd)
    return pl.pallas_call(
        functools.partial(_diff_kernel, tile=tile, nd=nd, nq=nq),
        grid=(b, DIFF_HEADS, nq // 2),
        in_specs=[
            pl.BlockSpec(memory_space=pltpu.SMEM),
            pl.BlockSpec(lam_vecs.shape, lambda bi, h, i: (0, 0)),
            pl.BlockSpec((1, hd), lambda bi, h, i: (0, 0)),
            *_paired_query_specs(tile, hd, nq, lambda h: h),
            pl.BlockSpec((1, s, hd), lambda bi, h, i: (bi, 0, DIFF_HEADS + h)),
            pl.BlockSpec((1, s, hd), lambda bi, h, i: (bi, 0, 2 * DIFF_HEADS + h)),
            pl.BlockSpec((1, nd, tile, tile), lambda bi, h, i: (h, 0, 0, 0)),
        ],
        out_specs=out_spec,
        out_shape=out_shape,
        scratch_shapes=[pltpu.VMEM((2, 2, tile, hd), BF16)] + _score_scratch(2, tile)
        + _stat_scratch(tile, hd) + _stat_scratch(tile, hd),
        compiler_params=_cparams("parallel", "parallel", "arbitrary"),
        name="diff_attention",
    )(scal, lam_vecs, subw, qkv, qkv, qkv, qkv, bias)


def _moba_kernel(qa_ref, qb_ref, k_ref, oh_ref, v_ref, bias_ref, o_ref,
                 kmean_ref, qaug_ref, s_ref, mx_ref, *stat_refs, tile, nd, nblk, heads, nq):
    blk = MOBA_BLOCK
    hd = MOBA_HEAD_DIM
    per_tile = tile // blk
    i = pl.program_id(2)
    stats = [stat_refs[2 * hh:2 * hh + 2] for hh in range(heads)]

    @pl.when(i == 0)
    def _():
        for hh in range(heads):
            for n in range(nblk):
                kb = k_ref[0, n * blk:(n + 1) * blk, hh * hd:(hh + 1) * hd].astype(F32)
                mean = jnp.mean(kb, axis=0, keepdims=True)
                for part in range(3):
                    term = mean.astype(BF16).astype(F32)
                    kmean_ref[hh, part * nblk + n:part * nblk + n + 1, :] = term
                    mean = mean - term

    for hh, (qt, q_ref) in itertools.product(range(heads), enumerate((qa_ref, qb_ref))):
        q = q_ref[0, :, hh * hd:(hh + 1) * hd]
        q_tile = i if qt == 0 else nq - 1 - i
        parts = _dot_nt(kmean_ref[hh].astype(BF16), q)
        gate = parts[:nblk] + parts[nblk:2 * nblk] + parts[2 * nblk:]
        blk_id = lax.broadcasted_iota(jnp.int32, gate.shape, 0)
        blk_f = blk_id.astype(F32)
        lane = lax.broadcasted_iota(jnp.int32, gate.shape, 1)
        own = q_tile * per_tile + lax.shift_right_logical(lane, int(math.log2(blk)))
        valid = blk_id < own
        gate = jnp.where(valid, gate, NEG_INF)
        sel = blk_id == own
        for _ in range(MOBA_TOPK):
            top = jnp.max(gate, axis=0, keepdims=True)
            first = jnp.min(jnp.where(gate == top, blk_f, float(nblk)), axis=0, keepdims=True)
            hit = blk_f == first
            sel = jnp.logical_or(sel, jnp.logical_and(hit, valid))
            gate = jnp.where(hit, -3.0e38, gate)
        pen = jnp.where(sel, 0.0, NEG_INF)
        pen = jnp.concatenate([pen, jnp.zeros((LANES - nblk, tile), F32)], axis=0)
        qaug_ref[qt, hh, :, :LANES] = q
        qaug_ref[qt, hh, :, LANES:] = pen.T.astype(BF16)
    for st in stats:
        _init_stats(*st)

    def fill(qt, j, dist, slot, hh):
        rows = _key_rows(j, tile)
        kaug = jnp.concatenate([k_ref[0, rows, hh * hd:(hh + 1) * hd], oh_ref[rows, :]], axis=1)
        bias = bias_ref[hh, jnp.minimum(dist, nd - 1)]
        _store_scores(_dot_nt(kaug, qaug_ref[qt, hh]) + bias, s_ref.at[slot, hh], mx_ref.at[slot, hh])

    def values(j):
        rows = _key_rows(j, tile)
        return [_values_t(v_ref[0, rows, hh * hd:(hh + 1) * hd]) for hh in range(heads)]

    def consume(qt, vt, slot, hh):
        m_ref, acc_ref = stats[hh]
        _softmax_update(s_ref.at[slot, hh], mx_ref.at[slot, hh], vt, m_ref.at[qt], acc_ref.at[qt])

    _paired_pipeline(i, nq, heads, fill, values, consume)
    for hh, qt in itertools.product(range(heads), range(2)):
        o_ref[0, 0, qt, :, hh * hd:(hh + 1) * hd] = _normalized(stats[hh][1], qt, hd).T.astype(o_ref.dtype)


def _moba_attention(qkv, bias, tile):
    b, s, _ = qkv.shape
    nd = bias.shape[1]
    nblk = s // MOBA_BLOCK
    hd = MOBA_HEAD_DIM
    hg = MOBA_HEADS_PER_STEP
    w = hg * hd
    col0 = 3 * DIFF_HEADS // hg
    groups = MOBA_HEADS // hg
    onehot = (jnp.arange(s)[:, None] // MOBA_BLOCK == jnp.arange(LANES)[None, :]).astype(BF16)
    nq = s // tile
    out_spec, out_shape = _paired_output(b, s, tile, w, MOBA_HEADS * hd)
    return pl.pallas_call(
        functools.partial(_moba_kernel, tile=tile, nd=nd, nblk=nblk, heads=hg, nq=nq),
        grid=(b, groups, nq // 2),
        in_specs=[
            *_paired_query_specs(tile, w, nq, lambda h: col0 + h),
            pl.BlockSpec((1, s, w), lambda bi, h, i: (bi, 0, col0 + groups + h)),
            pl.BlockSpec((s, LANES), lambda bi, h, i: (0, 0)),
            pl.BlockSpec((1, s, w), lambda bi, h, i: (bi, 0, col0 + 2 * groups + h)),
            pl.BlockSpec((hg, nd, tile, tile), lambda bi, h, i: (DIFF_HEADS // hg + h, 0, 0, 0)),
        ],
        out_specs=out_spec,
        out_shape=out_shape,
        scratch_shapes=[pltpu.VMEM((hg, 3 * nblk, hd), F32), pltpu.VMEM((2, hg, tile, 2 * LANES), BF16)]
        + _score_scratch(hg, tile) + hg * _stat_scratch(tile, hd),
        compiler_params=_cparams("parallel", "parallel", "arbitrary"),
        name="moba_attention",
    )(qkv, qkv, qkv, onehot, qkv, bias)


def _mla_prep_kernel(x_ref, nw_ref, wsmall_ref, qnw_ref, kvnw_ref, wuq_ref, wukv_ref, ct_ref, st_ref,
                     q_ref, k_ref, v_ref, *, scale):
    kw = MLA_HEADS * LANES
    ct = ct_ref[...]
    st = st_ref[...]
    small = _dot(_rms(x_ref[...], nw_ref[...]).astype(BF16), wsmall_ref[...])
    cq = small[:, :MLA_Q_RANK]
    ckv = small[:, MLA_Q_RANK:MLA_Q_RANK + MLA_KV_RANK]
    kpe = small[:, MLA_Q_RANK + MLA_KV_RANK:MLA_Q_RANK + MLA_KV_RANK + LANES]
    kpe_rot = small[:, MLA_Q_RANK + MLA_KV_RANK + LANES:]
    qq = _dot(_rms(cq, qnw_ref[...]).astype(BF16), wuq_ref[...])
    kk = _dot(_rms(ckv, kvnw_ref[...]).astype(BF16), wukv_ref[...])
    k_rope = kpe * ct + kpe_rot * st
    for h in range(MLA_HEADS):
        cols = slice(h * LANES, (h + 1) * LANES)
        rot = qq[:, kw + h * LANES:kw + (h + 1) * LANES]
        q_ref[:, cols] = ((qq[:, cols] * ct + rot * st) * scale).astype(BF16)
        k_ref[:, cols] = (kk[:, cols] + k_rope).astype(BF16)
    v_ref[...] = kk[:, kw:].astype(BF16)


def _mla_prep(x, nw, wsmall, qnw, kvnw, wuq, wukv, ct, st, tm):
    n, d = x.shape
    kw = MLA_HEADS * LANES
    vw = MLA_HEADS * MLA_V_DIM
    scale = LOG2E / math.sqrt(MLA_NOPE_DIM + MLA_ROPE_DIM)
    row = lambda i: (i, 0)
    const = lambda i: (0, 0)
    return pl.pallas_call(
        functools.partial(_mla_prep_kernel, scale=scale),
        grid=(n // tm,),
        in_specs=[
            pl.BlockSpec((tm, d), row),
            pl.BlockSpec(nw.shape, const),
            pl.BlockSpec(wsmall.shape, const),
            pl.BlockSpec(qnw.shape, const),
            pl.BlockSpec(kvnw.shape, const),
            pl.BlockSpec(wuq.shape, const),
            pl.BlockSpec(wukv.shape, const),
            pl.BlockSpec((tm, LANES), row),
            pl.BlockSpec((tm, LANES), row),
        ],
        out_specs=[pl.BlockSpec((tm, kw), row), pl.BlockSpec((tm, kw), row), pl.BlockSpec((tm, vw), row)],
        out_shape=[jax.ShapeDtypeStruct((n, kw), BF16), jax.ShapeDtypeStruct((n, kw), BF16),
                   jax.ShapeDtypeStruct((n, vw), BF16)],
        compiler_params=_cparams("parallel"),
        name="mla_prep",
    )(x, nw, wsmall, qnw, kvnw, wuq, wukv, ct, st)


def _mla_kernel(qa_ref, qb_ref, k_ref, v_ref, o_ref, q_ref, mask_ref, s_ref, mx_ref, *stat_refs,
                tile, heads, nq):
    i = pl.program_id(2)
    stats = [stat_refs[2 * hh:2 * hh + 2] for hh in range(heads)]
    for st in stats:
        _init_stats(*st)
    q_ref[0] = qa_ref[0]
    q_ref[1] = qb_ref[0]
    key = lax.broadcasted_iota(jnp.int32, (tile, tile), 0)
    qry = lax.broadcasted_iota(jnp.int32, (tile, tile), 1)
    mask_ref[0] = jnp.where(qry >= key, 0.0, NEG_INF)
    mask_ref[1] = jnp.zeros((tile, tile), F32)

    def fill(qt, j, dist, slot, hh):
        rows = _key_rows(j, tile)
        head = slice(hh * LANES, (hh + 1) * LANES)
        mask = mask_ref[jnp.minimum(dist, 1)]
        _store_scores(_dot_nt(k_ref[0, rows, head], q_ref[qt, :, head]) + mask,
                      s_ref.at[slot, hh], mx_ref.at[slot, hh])

    def values(j):
        rows = _key_rows(j, tile)
        ones = jnp.ones((SUM_ROWS, tile), BF16)
        out = []
        for pr in range(heads // 2):
            vt = v_ref[0, rows, pr * LANES:(pr + 1) * LANES].T
            out += [jnp.concatenate([vt[:MLA_V_DIM], ones], axis=0), jnp.concatenate([vt[MLA_V_DIM:], ones], axis=0)]
        return out

    def consume(qt, vt, slot, hh):
        m_ref, acc_ref = stats[hh]
        _softmax_update(s_ref.at[slot, hh], mx_ref.at[slot, hh], vt, m_ref.at[qt], acc_ref.at[qt])

    _paired_pipeline(i, nq, heads, fill, values, consume)
    for pr, qt in itertools.product(range(heads // 2), range(2)):
        o = jnp.concatenate([_normalized(stats[2 * pr][1], qt, MLA_V_DIM),
                             _normalized(stats[2 * pr + 1][1], qt, MLA_V_DIM)], axis=0)
        o_ref[0, 0, qt, :, pr * LANES:(pr + 1) * LANES] = o.T.astype(o_ref.dtype)


def _mla_attention(q, k, v, tile):
    b, s, _ = q.shape
    hg = MLA_HEADS_PER_STEP
    groups = MLA_HEADS // hg
    nq = s // tile
    out_spec, out_shape = _paired_output(b, s, tile, hg * MLA_V_DIM, MLA_HEADS * MLA_V_DIM)
    return pl.pallas_call(
        functools.partial(_mla_kernel, tile=tile, heads=hg, nq=nq),
        grid=(b, groups, nq // 2),
        in_specs=[
            *_paired_query_specs(tile, hg * LANES, nq, lambda h: h),
            pl.BlockSpec((1, s, hg * LANES), lambda bi, h, i: (bi, 0, h)),
            pl.BlockSpec((1, s, hg * MLA_V_DIM), lambda bi, h, i: (bi, 0, h)),
        ],
        out_specs=out_spec,
        out_shape=out_shape,
        scratch_shapes=[pltpu.VMEM((2, tile, hg * LANES), BF16), pltpu.VMEM((2, tile, tile), F32)]
        + _score_scratch(hg, tile) + hg * _stat_scratch(tile, MLA_V_DIM),
        compiler_params=_cparams("parallel", "parallel", "arbitrary"),
        name="mla_attention",
    )(q, q, k, v)


def _merge_kernel(x_ref, oa_ref, ob_ref, oc_ref, g_ref, wb_ref, wo_ref, nw_ref, o_ref):
    d = x_ref.shape[1]
    mixed = jnp.zeros(x_ref.shape, F32)
    for gi, br_ref in enumerate((oa_ref, ob_ref, oc_ref)):
        br = _dot(br_ref[0, 0, 0], wb_ref[gi])
        gate = 1.0 / (1.0 + jnp.exp(-g_ref[:, gi * d:(gi + 1) * d].astype(F32)))
        mixed = mixed + gate * br
    y = _dot(mixed.astype(BF16), wo_ref[...])
    o_ref[...] = x_ref[...] + _rms(y, nw_ref[...])


def _merge(x, oa, ob, oc, qkvg, wb, wo, nw):
    n, d = x.shape
    _, half, _, tm, _ = oa.shape
    branch = _pair_order_rows(2 * half, tm, BRANCH_WIDTH)
    g_block = qkvg.shape[1] // (N_BRANCH * d) - 1
    row = lambda i: (i, 0)
    return pl.pallas_call(
        _merge_kernel,
        grid=(n // tm,),
        in_specs=[
            pl.BlockSpec((tm, d), row),
            branch, branch, branch,
            pl.BlockSpec((tm, N_BRANCH * d), lambda i: (i, g_block)),
            pl.BlockSpec(wb.shape, lambda i: (0, 0, 0)),
            pl.BlockSpec(wo.shape, lambda i: (0, 0)),
            pl.BlockSpec((1, d), lambda i: (0, 0)),
        ],
        out_specs=pl.BlockSpec((tm, d), row),
        out_shape=jax.ShapeDtypeStruct((n, d), F32),
        compiler_params=_cparams("parallel"),
        name="branch_merge",
    )(x, oa, ob, oc, qkvg, wb, wo, nw)


def _mlp_kernel(x_ref, nw1_ref, wu_ref, wd_ref, nw2_ref, o_ref, h_ref, acc_ref):
    j = pl.program_id(1)

    @pl.when(j == 0)
    def _():
        h_ref[...] = _rms(x_ref[...], nw1_ref[...]).astype(BF16)
        acc_ref[...] = jnp.zeros(acc_ref.shape, F32)

    u = jnp.square(jnp.maximum(_dot(h_ref[...], wu_ref[...]), 0.0))
    acc_ref[...] += _dot(u.astype(BF16), wd_ref[...])

    @pl.when(j == pl.num_programs(1) - 1)
    def _():
        o_ref[...] = x_ref[...] + _rms(acc_ref[...], nw2_ref[...])


def _mlp(x, nw1, wu, wd, nw2, tm, tf):
    n, d = x.shape
    f = wu.shape[1]
    return pl.pallas_call(
        _mlp_kernel,
        grid=(n // tm, f // tf),
        in_specs=[
            pl.BlockSpec((tm, d), lambda i, j: (i, 0)),
            pl.BlockSpec((1, d), lambda i, j: (0, 0)),
            pl.BlockSpec((d, tf), lambda i, j: (0, j)),
            pl.BlockSpec((tf, d), lambda i, j: (j, 0)),
            pl.BlockSpec((1, d), lambda i, j: (0, 0)),
        ],
        out_specs=pl.BlockSpec((tm, d), lambda i, j: (i, 0)),
        out_shape=jax.ShapeDtypeStruct((n, d), F32),
        scratch_shapes=[pltpu.VMEM((tm, d), BF16), pltpu.VMEM((tm, d), F32)],
        compiler_params=_cparams("parallel", "arbitrary"),
        name="relu2_mlp",
    )(x, nw1, wu, wd, nw2)


def _rotate_half_cols(w):
    half = w.shape[-1] // 2
    return jnp.concatenate([-w[..., half:], w[..., :half]], axis=-1)


def _pad_cols(w, before, total):
    pad = [(0, 0)] * (w.ndim - 1) + [(before, total - before - w.shape[-1])]
    return jnp.pad(w, pad)


def _layer_weights(w_in, w_uq, w_ukv):
    n_qkv = 3 * DIFF_HEADS * 2 * DIFF_HEAD_DIM + 3 * MOBA_HEADS * MOBA_HEAD_DIM
    o_cq = n_qkv
    o_ckv = o_cq + MLA_Q_RANK
    o_kpe = o_ckv + MLA_KV_RANK
    o_g = o_kpe + MLA_ROPE_DIM
    w_qkvg = jnp.concatenate([w_in[:, :n_qkv], w_in[:, o_g:]], axis=1).astype(BF16)
    w_kpe = w_in[:, o_kpe:o_g]
    w_small = jnp.concatenate([
        w_in[:, o_cq:o_kpe],
        _pad_cols(w_kpe, MLA_NOPE_DIM, LANES),
        _pad_cols(_rotate_half_cols(w_kpe), MLA_NOPE_DIM, LANES)], axis=1).astype(BF16)

    qk_dim = MLA_NOPE_DIM + MLA_ROPE_DIM
    uq = w_uq.reshape(MLA_Q_RANK, MLA_HEADS, qk_dim)
    uq_plain = _pad_cols(uq, 0, LANES).reshape(MLA_Q_RANK, MLA_HEADS * LANES)
    uq_rot = _pad_cols(_rotate_half_cols(uq[..., MLA_NOPE_DIM:]), MLA_NOPE_DIM, LANES)
    wuq = jnp.concatenate([uq_plain, uq_rot.reshape(MLA_Q_RANK, MLA_HEADS * LANES)], axis=1).astype(BF16)

    ukv = w_ukv.reshape(MLA_KV_RANK, MLA_HEADS, MLA_NOPE_DIM + MLA_V_DIM)
    uk = _pad_cols(ukv[..., :MLA_NOPE_DIM], 0, LANES).reshape(MLA_KV_RANK, MLA_HEADS * LANES)
    uv = ukv[..., MLA_NOPE_DIM:].reshape(MLA_KV_RANK, MLA_HEADS * MLA_V_DIM)
    wukv = jnp.concatenate([uk, uv], axis=1).astype(BF16)
    return w_qkvg, w_small, wuq, wukv


def kernel(x, positions, rel_bias, norm_mix_pre, norm_mix_post, norm_mlp_pre, norm_mlp_post, w_in, diff_lambda, diff_subln, mla_q_norm, mla_w_uq, mla_kv_norm, mla_w_ukv, w_branch, w_out, w_up, w_down):
    b, s, d = x.shape
    n = b * s
    depth = w_in.shape[0]
    tile = ATTN_TILE
    assert tile % MOBA_BLOCK == 0 and n % 2048 == 0 and (s // MOBA_BLOCK) % 8 == 0
    assert s % (tile * PIPELINE_UNROLL) == 0 and PIPELINE_UNROLL % 2 == 0

    xf = x.reshape(n, d)
    bias = _bias_tiles(rel_bias, tile)

    half = MLA_ROPE_DIM // 2
    inv_freq = ROPE_THETA ** (-jnp.arange(half, dtype=F32) * 2.0 / MLA_ROPE_DIM)
    ang = positions.astype(F32).reshape(n, 1) * inv_freq
    cos, sin = jnp.cos(ang), jnp.sin(ang)
    ct = jnp.concatenate([jnp.ones((n, MLA_NOPE_DIM), F32), cos, cos,
                          jnp.zeros((n, LANES - MLA_NOPE_DIM - MLA_ROPE_DIM), F32)], axis=1)
    st = jnp.concatenate([jnp.zeros((n, MLA_NOPE_DIM), F32), sin, sin,
                          jnp.zeros((n, LANES - MLA_NOPE_DIM - MLA_ROPE_DIM), F32)], axis=1)

    n_diff = DIFF_HEADS * 2 * DIFF_HEAD_DIM
    n_moba = MOBA_HEADS * MOBA_HEAD_DIM
    qkv_scale = jnp.concatenate([
        jnp.full((n_diff,), LOG2E / math.sqrt(DIFF_HEAD_DIM), F32), jnp.ones((2 * n_diff,), F32),
        jnp.full((n_moba,), LOG2E / math.sqrt(MOBA_HEAD_DIM), F32), jnp.ones((2 * n_moba,), F32),
        jnp.ones((N_BRANCH * d,), F32)]).reshape(1, -1)

    row = lambda a: a.reshape(1, -1)
    for l in range(depth):
        lam_init = 0.8 - 0.6 * math.exp(-0.3 * l)
        w_qkvg, w_small, wuq, wukv = _layer_weights(w_in[l], mla_w_uq[l], mla_w_ukv[l])
        nw = row(norm_mix_pre[l])
        qkvg = _norm_matmul(xf, nw, w_qkvg, qkv_scale, BF16, 2048, 1024)

        qkv3 = qkvg.reshape(b, s, -1)
        scal = jnp.full((1,), lam_init, F32)
        oa = _diff_attention(qkv3, bias, scal, diff_lambda[l], row(diff_subln[l]), tile)
        ob = _moba_attention(qkv3, bias, tile)
        mq, mk, mv = _mla_prep(xf, nw, w_small, row(mla_q_norm[l]), row(mla_kv_norm[l]), wuq, wukv, ct, st, 512)
        oc = _mla_attention(mq.reshape(b, s, -1), mk.reshape(b, s, -1), mv.reshape(b, s, -1), tile)

        xf = _merge(xf, oa, ob, oc, qkvg, w_branch[l].astype(BF16), w_out[l].astype(BF16), row(norm_mix_post[l]))
        xf = _mlp(xf, row(norm_mlp_pre[l]), w_up[l].astype(BF16), w_down[l].astype(BF16),
                  row(norm_mlp_post[l]), 1024, 1024)
    return xf.reshape(b, s, d)
```

```python
import functools
import itertools
import math

import jax
import jax.numpy as jnp
from jax import lax
from jax.experimental import pallas as pl
from jax.experimental.pallas import tpu as pltpu

F32 = jnp.float32
BF16 = jnp.bfloat16

RMS_EPS = 1e-6
NEG_INF = -1e30
STAT_INIT = -(2.0 ** 100)

T5_BUCKETS = 32
T5_MAX_EXACT = T5_BUCKETS // 2
T5_MAX_DISTANCE = 1024

DIFF_HEADS = 4
DIFF_HEAD_DIM = 64
MOBA_HEADS = 4
MOBA_HEAD_DIM = 128
MOBA_BLOCK = 256
MOBA_TOPK = 3
MLA_HEADS = 8
MLA_Q_RANK = 256
MLA_KV_RANK = 128
MLA_NOPE_DIM = 64
MLA_ROPE_DIM = 32
MLA_V_DIM = 64
ROPE_THETA = 10000.0
N_BRANCH = 3
BRANCH_WIDTH = 512

LOG2E = math.log2(math.e)
DIFF_HEADS_PER_STEP = 2
MOBA_HEADS_PER_STEP = 2
MLA_HEADS_PER_STEP = 4
PIPELINE_UNROLL = 4
SUM_ROWS = 16
LANES = 128
ATTN_TILE = 512
VMEM_LIMIT = 48 * 1024 * 1024


def _cparams(*sem):
    return pltpu.CompilerParams(dimension_semantics=sem, vmem_limit_bytes=VMEM_LIMIT)


def _rms(x, w):
    return x * lax.rsqrt(jnp.mean(x * x, axis=-1, keepdims=True) + RMS_EPS) * w


def _dot(a, b):
    return jnp.dot(a, b, preferred_element_type=F32)


def _dot_nt(a, b):
    return lax.dot_general(a, b, (((1,), (1,)), ((), ())), preferred_element_type=F32)


def _norm_matmul_kernel(x_ref, nw_ref, w_ref, cs_ref, o_ref, h_ref):
    @pl.when(pl.program_id(1) == 0)
    def _():
        h_ref[...] = _rms(x_ref[...], nw_ref[...]).astype(BF16)

    o_ref[...] = (_dot(h_ref[...], w_ref[...]) * cs_ref[...]).astype(o_ref.dtype)


def _norm_matmul(x, nw, w, colscale, out_dtype, tm, tn):
    n, d = x.shape
    nout = w.shape[1]
    return pl.pallas_call(
        _norm_matmul_kernel,
        grid=(n // tm, nout // tn),
        in_specs=[
            pl.BlockSpec((tm, d), lambda i, j: (i, 0)),
            pl.BlockSpec((1, d), lambda i, j: (0, 0)),
            pl.BlockSpec((d, tn), lambda i, j: (0, j)),
            pl.BlockSpec((1, tn), lambda i, j: (0, j)),
        ],
        out_specs=pl.BlockSpec((tm, tn), lambda i, j: (i, j)),
        out_shape=jax.ShapeDtypeStruct((n, nout), out_dtype),
        scratch_shapes=[pltpu.VMEM((tm, d), BF16)],
        compiler_params=_cparams("parallel", "arbitrary"),
        name="norm_matmul",
    )(x, nw, w, colscale)


def _bias_tile_kernel(tab_ref, o_ref, *, tile):
    h = pl.program_id(0)
    d = pl.program_id(1)
    c = lax.broadcasted_iota(jnp.int32, (tile, tile), 0)
    r = lax.broadcasted_iota(jnp.int32, (tile, tile), 1)
    rel = d * tile + r - c
    n = jnp.maximum(rel, 0)
    nf = jnp.maximum(n, 1).astype(F32)
    large = T5_MAX_EXACT + (jnp.log(nf / T5_MAX_EXACT)
                            / math.log(T5_MAX_DISTANCE / T5_MAX_EXACT)
                            * (T5_BUCKETS - T5_MAX_EXACT)).astype(jnp.int32)
    large = jnp.minimum(large, T5_BUCKETS - 1)
    bucket = jnp.where(n < T5_MAX_EXACT, n, large)
    val = jnp.zeros((tile, tile), F32)
    for b in range(T5_BUCKETS):
        val = jnp.where(bucket == b, tab_ref[b, h], val)
    o_ref[0, 0] = jnp.where(rel >= 0, val * LOG2E, NEG_INF)


def _num_bias_tiles(tile):
    return -(-(T5_MAX_DISTANCE + tile - 1) // tile) + 1


def _bias_tiles(rel_bias, tile):
    nh = rel_bias.shape[1]
    nd = _num_bias_tiles(tile)
    return pl.pallas_call(
        functools.partial(_bias_tile_kernel, tile=tile),
        grid=(nh, nd),
        in_specs=[pl.BlockSpec(memory_space=pltpu.SMEM)],
        out_specs=pl.BlockSpec((1, 1, tile, tile), lambda h, d: (h, d, 0, 0)),
        out_shape=jax.ShapeDtypeStruct((nh, nd, tile, tile), F32),
        compiler_params=_cparams("parallel", "parallel"),
        name="t5_bias_tiles",
    )(rel_bias)


def _store_scores(s, s_ref, mx_ref):
    s_ref[...] = s.astype(BF16)
    mx_ref[...] = jnp.max(s, axis=0, keepdims=True).astype(BF16).astype(F32)


def _softmax_update(s_ref, mx_ref, vt, m_ref, acc_ref):
    m_prev = m_ref[...]
    m_new = jnp.maximum(m_prev, mx_ref[...])
    alpha = jnp.exp2(m_prev - m_new)
    p = jnp.exp2(s_ref[...] - m_new.astype(BF16))
    acc_ref[...] = alpha * acc_ref[...] + _dot(vt, p)
    m_ref[...] = m_new


def _paired_pipeline(i, nq, n_maps, fill, values, consume):
    def item(t):
        second = (jnp.int32(t) > i).astype(jnp.int32)
        j = t - second * (i + 1)
        dist = i + second * (nq - 1 - 2 * i) - j
        return second, j, dist

    def fill_item(t, slot):
        qt, j, dist = item(t)
        for mp in range(n_maps):
            fill(qt, j, dist, slot, mp)

    def step(t, slot):
        qt_f, j_f, dist_f = item(t + 1)
        qt_u, j_u, _ = item(t)
        vts = values(j_u)
        for mp in range(n_maps):
            fill(qt_f, j_f, dist_f, 1 - slot, mp)
            consume(qt_u, vts[mp], slot, mp)

    fill_item(0, 0)

    def trip(r, c):
        for u in range(PIPELINE_UNROLL):
            step(r * PIPELINE_UNROLL + u, u % 2)
        return c

    lax.fori_loop(0, nq // PIPELINE_UNROLL, trip, 0)
    qt, j, _ = item(nq)
    vts = values(j)
    for mp in range(n_maps):
        consume(qt, vts[mp], 0, mp)


def _values_t(v):
    return jnp.concatenate([v.T, jnp.ones((SUM_ROWS, v.shape[0]), v.dtype)], axis=0)


def _init_stats(m_ref, acc_ref):
    m_ref[...] = jnp.full(m_ref.shape, STAT_INIT, F32)
    acc_ref[...] = jnp.zeros(acc_ref.shape, F32)


def _stat_scratch(tile, dv):
    return [pltpu.VMEM((2, 1, tile), F32), pltpu.VMEM((2, dv + SUM_ROWS, tile), F32)]


def _score_scratch(n_maps, tile):
    return [pltpu.VMEM((2, n_maps, tile, tile), BF16), pltpu.VMEM((2, n_maps, 1, tile), F32)]


def _normalized(acc_ref, qt, dv):
    return acc_ref[qt, :dv, :] / acc_ref[qt, dv:dv + 1, :]


def _paired_query_specs(tile, width, nq, col_of):
    return [pl.BlockSpec((1, tile, width), lambda bi, h, i: (bi, i, col_of(h))),
            pl.BlockSpec((1, tile, width), lambda bi, h, i: (bi, nq - 1 - i, col_of(h)))]


def _paired_output(b, s, tile, width, total_width):
    spec = pl.BlockSpec((1, 1, 2, tile, width), lambda bi, h, i: (bi, i, 0, 0, h))
    return spec, jax.ShapeDtypeStruct((b, s // tile // 2, 2, tile, total_width), BF16)


def _pair_order_rows(nq, tile, width):
    def index(r):
        t = lax.rem(r, nq)
        return lax.div(r, nq), jnp.minimum(t, nq - 1 - t), (t >= nq // 2).astype(jnp.int32), 0, 0
    return pl.BlockSpec((1, 1, 1, tile, width), index)


def _key_rows(j, tile):
    return pl.ds(pl.multiple_of(j * tile, tile), tile)


def _diff_kernel(scal_ref, lam_ref, subw_ref, qa_ref, qb_ref, k_ref, v_ref, bias_ref, o_ref,
                 qm_ref, s_ref, mx_ref, *stat_refs, tile, nd, nq, heads):
    i = pl.program_id(2)
    hd = 2 * DIFF_HEAD_DIM
    n_maps = 2 * heads
    for qt, q_ref in enumerate((qa_ref, qb_ref)):
        for hh in range(heads):
            q = q_ref[0, :, hh * hd:(hh + 1) * hd]
            lane = lax.broadcasted_iota(jnp.int32, q.shape, 1)
            zero = jnp.zeros_like(q)
            qm_ref[qt, 2 * hh] = jnp.where(lane < DIFF_HEAD_DIM, q, zero)
            qm_ref[qt, 2 * hh + 1] = jnp.where(lane >= DIFF_HEAD_DIM, q, zero)
    stats = [stat_refs[2 * mp:2 * mp + 2] for mp in range(n_maps)]
    for st in stats:
        _init_stats(*st)

    def fill(qt, j, dist, slot, mp):
        hh = mp // 2
        k = k_ref[0, _key_rows(j, tile), hh * hd:(hh + 1) * hd]
        bias = bias_ref[hh, jnp.minimum(dist, nd - 1)]
        _store_scores(_dot_nt(k, qm_ref[qt, mp]) + bias, s_ref.at[slot, mp], mx_ref.at[slot, mp])

    def values(j):
        rows = _key_rows(j, tile)
        vts = [_values_t(v_ref[0, rows, hh * hd:(hh + 1) * hd]) for hh in range(heads)]
        return [vts[mp // 2] for mp in range(n_maps)]

    def consume(qt, vt, slot, mp):
        m_ref, acc_ref = stats[mp]
        _softmax_update(s_ref.at[slot, mp], mx_ref.at[slot, mp], vt, m_ref.at[qt], acc_ref.at[qt])

    _paired_pipeline(i, nq, n_maps, fill, values, consume)

    lam_init = scal_ref[0]
    lv = lam_ref[...]
    lam = (jnp.exp(jnp.sum(lv[0:1] * lv[1:2], axis=1, keepdims=True))
           - jnp.exp(jnp.sum(lv[2:3] * lv[3:4], axis=1, keepdims=True)) + lam_init)
    for hh, qt in itertools.product(range(heads), range(2)):
        o = (_normalized(stats[2 * hh][1], qt, hd) - lam * _normalized(stats[2 * hh + 1][1], qt, hd)).T
        o_ref[0, 0, qt, :, hh * hd:(hh + 1) * hd] = (
            _rms(o, subw_ref[...]) * (1.0 - lam_init)).astype(o_ref.dtype)


def _diff_attention(qkv, bias, scal, lam_vecs, subw, tile):
    b, s, _ = qkv.shape
    nd = bias.shape[1]
    nq = s // tile
    hd = 2 * DIFF_HEAD_DIM
    hg = DIFF_HEADS_PER_STEP
    w = hg * hd
    groups = DIFF_HEADS // hg
    out_spec, out_shape = _paired_output(b, s, tile, w, DIFF_HEADS * hd)
    return pl.pallas_call(
        functools.partial(_diff_kernel, tile=tile, nd=nd, nq=nq, heads=hg),
        grid=(b, groups, nq // 2),
        in_specs=[
            pl.BlockSpec(memory_space=pltpu.SMEM),
            pl.BlockSpec(lam_vecs.shape, lambda bi, h, i: (0, 0)),
            pl.BlockSpec((1, hd), lambda bi, h, i: (0, 0)),
            *_paired_query_specs(tile, w, nq, lambda h: h),
            pl.BlockSpec((1, s, w), lambda bi, h, i: (bi, 0, groups + h)),
            pl.BlockSpec((1, s, w), lambda bi, h, i: (bi, 0, 2 * groups + h)),
            pl.BlockSpec((hg, nd, tile, tile), lambda bi, h, i: (h, 0, 0, 0)),
        ],
        out_specs=out_spec,
        out_shape=out_shape,
        scratch_shapes=[pltpu.VMEM((2, 2 * hg, tile, hd), BF16)] + _score_scratch(2 * hg, tile)
        + 2 * hg * _stat_scratch(tile, hd),
        compiler_params=_cparams("parallel", "parallel", "arbitrary"),
        name="diff_attention",
    )(scal, lam_vecs, subw, qkv, qkv, qkv, qkv, bias)


def _moba_kernel(qa_ref, qb_ref, k_ref, oh_ref, v_ref, bias_ref, o_ref,
                 kmean_ref, qaug_ref, s_ref, mx_ref, *stat_refs, tile, nd, nblk, heads, nq):
    blk = MOBA_BLOCK
    hd = MOBA_HEAD_DIM
    per_tile = tile // blk
    i = pl.program_id(2)
    stats = [stat_refs[2 * hh:2 * hh + 2] for hh in range(heads)]

    @pl.when(i == 0)
    def _():
        for hh in range(heads):
            for n in range(nblk):
                kb = k_ref[0, n * blk:(n + 1) * blk, hh * hd:(hh + 1) * hd].astype(F32)
                mean = jnp.mean(kb, axis=0, keepdims=True)
                for part in range(3):
                    term = mean.astype(BF16).astype(F32)
                    kmean_ref[hh, part * nblk + n:part * nblk + n + 1, :] = term
                    mean = mean - term

    for hh, (qt, q_ref) in itertools.product(range(heads), enumerate((qa_ref, qb_ref))):
        q = q_ref[0, :, hh * hd:(hh + 1) * hd]
        q_tile = i if qt == 0 else nq - 1 - i
        parts = _dot_nt(kmean_ref[hh].astype(BF16), q)
        gate = parts[:nblk] + parts[nblk:2 * nblk] + parts[2 * nblk:]
        blk_id = lax.broadcasted_iota(jnp.int32, gate.shape, 0)
        blk_f = blk_id.astype(F32)
        lane = lax.broadcasted_iota(jnp.int32, gate.shape, 1)
        own = q_tile * per_tile + lax.shift_right_logical(lane, int(math.log2(blk)))
        valid = blk_id < own
        gate = jnp.where(valid, gate, NEG_INF)
        sel = blk_id == own
        for _ in range(MOBA_TOPK):
            top = jnp.max(gate, axis=0, keepdims=True)
            first = jnp.min(jnp.where(gate == top, blk_f, float(nblk)), axis=0, keepdims=True)
            hit = blk_f == first
            sel = jnp.logical_or(sel, jnp.logical_and(hit, valid))
            gate = jnp.where(hit, -3.0e38, gate)
        pen = jnp.where(sel, 0.0, NEG_INF)
        pen = jnp.concatenate([pen, jnp.zeros((LANES - nblk, tile), F32)], axis=0)
        qaug_ref[qt, hh, :, :LANES] = q
        qaug_ref[qt, hh, :, LANES:] = pen.T.astype(BF16)
    for st in stats:
        _init_stats(*st)

    def fill(qt, j, dist, slot, hh):
        rows = _key_rows(j, tile)
        kaug = jnp.concatenate([k_ref[0, rows, hh * hd:(hh + 1) * hd], oh_ref[rows, :]], axis=1)
        bias = bias_ref[hh, jnp.minimum(dist, nd - 1)]
        _store_scores(_dot_nt(kaug, qaug_ref[qt, hh]) + bias, s_ref.at[slot, hh], mx_ref.at[slot, hh])

    def values(j):
        rows = _key_rows(j, tile)
        return [_values_t(v_ref[0, rows, hh * hd:(hh + 1) * hd]) for hh in range(heads)]

    def consume(qt, vt, slot, hh):
        m_ref, acc_ref = stats[hh]
        _softmax_update(s_ref.at[slot, hh], mx_ref.at[slot, hh], vt, m_ref.at[qt], acc_ref.at[qt])

    _paired_pipeline(i, nq, heads, fill, values, consume)
    for hh, qt in itertools.product(range(heads), range(2)):
        o_ref[0, 0, qt, :, hh * hd:(hh + 1) * hd] = _normalized(stats[hh][1], qt, hd).T.astype(o_ref.dtype)


def _moba_attention(qkv, bias, tile):
    b, s, _ = qkv.shape
    nd = bias.shape[1]
    nblk = s // MOBA_BLOCK
    hd = MOBA_HEAD_DIM
    hg = MOBA_HEADS_PER_STEP
    w = hg * hd
    col0 = 3 * DIFF_HEADS // hg
    groups = MOBA_HEADS // hg
    onehot = (jnp.arange(s)[:, None] // MOBA_BLOCK == jnp.arange(LANES)[None, :]).astype(BF16)
    nq = s // tile
    out_spec, out_shape = _paired_output(b, s, tile, w, MOBA_HEADS * hd)
    return pl.pallas_call(
        functools.partial(_moba_kernel, tile=tile, nd=nd, nblk=nblk, heads=hg, nq=nq),
        grid=(b, groups, nq // 2),
        in_specs=[
            *_paired_query_specs(tile, w, nq, lambda h: col0 + h),
            pl.BlockSpec((1, s, w), lambda bi, h, i: (bi, 0, col0 + groups + h)),
            pl.BlockSpec((s, LANES), lambda bi, h, i: (0, 0)),
            pl.BlockSpec((1, s, w), lambda bi, h, i: (bi, 0, col0 + 2 * groups + h)),
            pl.BlockSpec((hg, nd, tile, tile), lambda bi, h, i: (DIFF_HEADS // hg + h, 0, 0, 0)),
        ],
        out_specs=out_spec,
        out_shape=out_shape,
        scratch_shapes=[pltpu.VMEM((hg, 3 * nblk, hd), F32), pltpu.VMEM((2, hg, tile, 2 * LANES), BF16)]
        + _score_scratch(hg, tile) + hg * _stat_scratch(tile, hd),
        compiler_params=_cparams("parallel", "parallel", "arbitrary"),
        name="moba_attention",
    )(qkv, qkv, qkv, onehot, qkv, bias)


def _mla_prep_kernel(x_ref, nw_ref, wsmall_ref, qnw_ref, kvnw_ref, wuq_ref, wukv_ref, ct_ref, st_ref,
                     q_ref, k_ref, v_ref, *, scale):
    kw = MLA_HEADS * LANES
    ct = ct_ref[...]
    st = st_ref[...]
    small = _dot(_rms(x_ref[...], nw_ref[...]).astype(BF16), wsmall_ref[...])
    cq = small[:, :MLA_Q_RANK]
    ckv = small[:, MLA_Q_RANK:MLA_Q_RANK + MLA_KV_RANK]
    kpe = small[:, MLA_Q_RANK + MLA_KV_RANK:MLA_Q_RANK + MLA_KV_RANK + LANES]
    kpe_rot = small[:, MLA_Q_RANK + MLA_KV_RANK + LANES:]
    qq = _dot(_rms(cq, qnw_ref[...]).astype(BF16), wuq_ref[...])
    kk = _dot(_rms(ckv, kvnw_ref[...]).astype(BF16), wukv_ref[...])
    k_rope = kpe * ct + kpe_rot * st
    for h in range(MLA_HEADS):
        cols = slice(h * LANES, (h + 1) * LANES)
        rot = qq[:, kw + h * LANES:kw + (h + 1) * LANES]
        q_ref[:, cols] = ((qq[:, cols] * ct + rot * st) * scale).astype(BF16)
        k_ref[:, cols] = (kk[:, cols] + k_rope).astype(BF16)
    v_ref[...] = kk[:, kw:].astype(BF16)


def _mla_prep(x, nw, wsmall, qnw, kvnw, wuq, wukv, ct, st, tm):
    n, d = x.shape
    kw = MLA_HEADS * LANES
    vw = MLA_HEADS * MLA_V_DIM
    scale = LOG2E / math.sqrt(MLA_NOPE_DIM + MLA_ROPE_DIM)
    row = lambda i: (i, 0)
    const = lambda i: (0, 0)
    return pl.pallas_call(
        functools.partial(_mla_prep_kernel, scale=scale),
        grid=(n // tm,),
        in_specs=[
            pl.BlockSpec((tm, d), row),
            pl.BlockSpec(nw.shape, const),
            pl.BlockSpec(wsmall.shape, const),
            pl.BlockSpec(qnw.shape, const),
            pl.BlockSpec(kvnw.shape, const),
            pl.BlockSpec(wuq.shape, const),
            pl.BlockSpec(wukv.shape, const),
            pl.BlockSpec((tm, LANES), row),
            pl.BlockSpec((tm, LANES), row),
        ],
        out_specs=[pl.BlockSpec((tm, kw), row), pl.BlockSpec((tm, kw), row), pl.BlockSpec((tm, vw), row)],
        out_shape=[jax.ShapeDtypeStruct((n, kw), BF16), jax.ShapeDtypeStruct((n, kw), BF16),
                   jax.ShapeDtypeStruct((n, vw), BF16)],
        compiler_params=_cparams("parallel"),
        name="mla_prep",
    )(x, nw, wsmall, qnw, kvnw, wuq, wukv, ct, st)


def _mla_kernel(qa_ref, qb_ref, k_ref, v_ref, o_ref, q_ref, mask_ref, s_ref, mx_ref, *stat_refs,
                tile, heads, nq):
    i = pl.program_id(2)
    stats = [stat_refs[2 * hh:2 * hh + 2] for hh in range(heads)]
    for st in stats:
        _init_stats(*st)
    q_ref[0] = qa_ref[0]
    q_ref[1] = qb_ref[0]
    key = lax.broadcasted_iota(jnp.int32, (tile, tile), 0)
    qry = lax.broadcasted_iota(jnp.int32, (tile, tile), 1)
    mask_ref[0] = jnp.where(qry >= key, 0.0, NEG_INF)
    mask_ref[1] = jnp.zeros((tile, tile), F32)

    def fill(qt, j, dist, slot, hh):
        rows = _key_rows(j, tile)
        head = slice(hh * LANES, (hh + 1) * LANES)
        mask = mask_ref[jnp.minimum(dist, 1)]
        _store_scores(_dot_nt(k_ref[0, rows, head], q_ref[qt, :, head]) + mask,
                      s_ref.at[slot, hh], mx_ref.at[slot, hh])

    def values(j):
        rows = _key_rows(j, tile)
        ones = jnp.ones((SUM_ROWS, tile), BF16)
        out = []
        for pr in range(heads // 2):
            vt = v_ref[0, rows, pr * LANES:(pr + 1) * LANES].T
            out += [jnp.concatenate([vt[:MLA_V_DIM], ones], axis=0), jnp.concatenate([vt[MLA_V_DIM:], ones], axis=0)]
        return out

    def consume(qt, vt, slot, hh):
        m_ref, acc_ref = stats[hh]
        _softmax_update(s_ref.at[slot, hh], mx_ref.at[slot, hh], vt, m_ref.at[qt], acc_ref.at[qt])

    _paired_pipeline(i, nq, heads, fill, values, consume)
    for pr, qt in itertools.product(range(heads // 2), range(2)):
        o = jnp.concatenate([_normalized(stats[2 * pr][1], qt, MLA_V_DIM),
                             _normalized(stats[2 * pr + 1][1], qt, MLA_V_DIM)], axis=0)
        o_ref[0, 0, qt, :, pr * LANES:(pr + 1) * LANES] = o.T.astype(o_ref.dtype)


def _mla_attention(q, k, v, tile):
    b, s, _ = q.shape
    hg = MLA_HEADS_PER_STEP
    groups = MLA_HEADS // hg
    nq = s // tile
    out_spec, out_shape = _paired_output(b, s, tile, hg * MLA_V_DIM, MLA_HEADS * MLA_V_DIM)
    return pl.pallas_call(
        functools.partial(_mla_kernel, tile=tile, heads=hg, nq=nq),
        grid=(b, groups, nq // 2),
        in_specs=[
            *_paired_query_specs(tile, hg * LANES, nq, lambda h: h),
            pl.BlockSpec((1, s, hg * LANES), lambda bi, h, i: (bi, 0, h)),
            pl.BlockSpec((1, s, hg * MLA_V_DIM), lambda bi, h, i: (bi, 0, h)),
        ],
        out_specs=out_spec,
        out_shape=out_shape,
        scratch_shapes=[pltpu.VMEM((2, tile, hg * LANES), BF16), pltpu.VMEM((2, tile, tile), F32)]
        + _score_scratch(hg, tile) + hg * _stat_scratch(tile, MLA_V_DIM),
        compiler_params=_cparams("parallel", "parallel", "arbitrary"),
        name="mla_attention",
    )(q, q, k, v)


def _merge_kernel(x_ref, oa_ref, ob_ref, oc_ref, g_ref, wb_ref, wo_ref, nw_ref, o_ref):
    d = x_ref.shape[1]
    mixed = jnp.zeros(x_ref.shape, F32)
    for gi, br_ref in enumerate((oa_ref, ob_ref, oc_ref)):
        br = _dot(br_ref[0, 0, 0], wb_ref[gi])
        gate = 1.0 / (1.0 + jnp.exp(-g_ref[:, gi * d:(gi + 1) * d].astype(F32)))
        mixed = mixed + gate * br
    y = _dot(mixed.astype(BF16), wo_ref[...])
    o_ref[...] = x_ref[...] + _rms(y, nw_ref[...])


def _merge(x, oa, ob, oc, qkvg, wb, wo, nw):
    n, d = x.shape
    _, half, _, tm, _ = oa.shape
    branch = _pair_order_rows(2 * half, tm, BRANCH_WIDTH)
    g_block = qkvg.shape[1] // (N_BRANCH * d) - 1
    row = lambda i: (i, 0)
    return pl.pallas_call(
        _merge_kernel,
        grid=(n // tm,),
        in_specs=[
            pl.BlockSpec((tm, d), row),
            branch, branch, branch,
            pl.BlockSpec((tm, N_BRANCH * d), lambda i: (i, g_block)),
            pl.BlockSpec(wb.shape, lambda i: (0, 0, 0)),
            pl.BlockSpec(wo.shape, lambda i: (0, 0)),
            pl.BlockSpec((1, d), lambda i: (0, 0)),
        ],
        out_specs=pl.BlockSpec((tm, d), row),
        out_shape=jax.ShapeDtypeStruct((n, d), F32),
        compiler_params=_cparams("parallel"),
        name="branch_merge",
    )(x, oa, ob, oc, qkvg, wb, wo, nw)


def _mlp_kernel(x_ref, nw1_ref, wu_ref, wd_ref, nw2_ref, o_ref, h_ref, acc_ref):
    j = pl.program_id(1)

    @pl.when(j == 0)
    def _():
        h_ref[...] = _rms(x_ref[...], nw1_ref[...]).astype(BF16)
        acc_ref[...] = jnp.zeros(acc_ref.shape, F32)

    u = jnp.square(jnp.maximum(_dot(h_ref[...], wu_ref[...]), 0.0))
    acc_ref[...] += _dot(u.astype(BF16), wd_ref[...])

    @pl.when(j == pl.num_programs(1) - 1)
    def _():
        o_ref[...] = x_ref[...] + _rms(acc_ref[...], nw2_ref[...])


def _mlp(x, nw1, wu, wd, nw2, tm, tf):
    n, d = x.shape
    f = wu.shape[1]
    return pl.pallas_call(
        _mlp_kernel,
        grid=(n // tm, f // tf),
        in_specs=[
            pl.BlockSpec((tm, d), lambda i, j: (i, 0)),
            pl.BlockSpec((1, d), lambda i, j: (0, 0)),
            pl.BlockSpec((d, tf), lambda i, j: (0, j)),
            pl.BlockSpec((tf, d), lambda i, j: (j, 0)),
            pl.BlockSpec((1, d), lambda i, j: (0, 0)),
        ],
        out_specs=pl.BlockSpec((tm, d), lambda i, j: (i, 0)),
        out_shape=jax.ShapeDtypeStruct((n, d), F32),
        scratch_shapes=[pltpu.VMEM((tm, d), BF16), pltpu.VMEM((tm, d), F32)],
        compiler_params=_cparams("parallel", "arbitrary"),
        name="relu2_mlp",
    )(x, nw1, wu, wd, nw2)


def _rotate_half_cols(w):
    half = w.shape[-1] // 2
    return jnp.concatenate([-w[..., half:], w[..., :half]], axis=-1)


def _pad_cols(w, before, total):
    pad = [(0, 0)] * (w.ndim - 1) + [(before, total - before - w.shape[-1])]
    return jnp.pad(w, pad)


def _layer_weights(w_in, w_uq, w_ukv):
    n_qkv = 3 * DIFF_HEADS * 2 * DIFF_HEAD_DIM + 3 * MOBA_HEADS * MOBA_HEAD_DIM
    o_cq = n_qkv
    o_ckv = o_cq + MLA_Q_RANK
    o_kpe = o_ckv + MLA_KV_RANK
    o_g = o_kpe + MLA_ROPE_DIM
    w_qkvg = jnp.concatenate([w_in[:, :n_qkv], w_in[:, o_g:]], axis=1).astype(BF16)
    w_kpe = w_in[:, o_kpe:o_g]
    w_small = jnp.concatenate([
        w_in[:, o_cq:o_kpe],
        _pad_cols(w_kpe, MLA_NOPE_DIM, LANES),
        _pad_cols(_rotate_half_cols(w_kpe), MLA_NOPE_DIM, LANES)], axis=1).astype(BF16)

    qk_dim = MLA_NOPE_DIM + MLA_ROPE_DIM
    uq = w_uq.reshape(MLA_Q_RANK, MLA_HEADS, qk_dim)
    uq_plain = _pad_cols(uq, 0, LANES).reshape(MLA_Q_RANK, MLA_HEADS * LANES)
    uq_rot = _pad_cols(_rotate_half_cols(uq[..., MLA_NOPE_DIM:]), MLA_NOPE_DIM, LANES)
    wuq = jnp.concatenate([uq_plain, uq_rot.reshape(MLA_Q_RANK, MLA_HEADS * LANES)], axis=1).astype(BF16)

    ukv = w_ukv.reshape(MLA_KV_RANK, MLA_HEADS, MLA_NOPE_DIM + MLA_V_DIM)
    uk = _pad_cols(ukv[..., :MLA_NOPE_DIM], 0, LANES).reshape(MLA_KV_RANK, MLA_HEADS * LANES)
    uv = ukv[..., MLA_NOPE_DIM:].reshape(MLA_KV_RANK, MLA_HEADS * MLA_V_DIM)
    wukv = jnp.concatenate([uk, uv], axis=1).astype(BF16)
    return w_qkvg, w_small, wuq, wukv


def kernel(x, positions, rel_bias, norm_mix_pre, norm_mix_post, norm_mlp_pre, norm_mlp_post, w_in, diff_lambda, diff_subln, mla_q_norm, mla_w_uq, mla_kv_norm, mla_w_ukv, w_branch, w_out, w_up, w_down):
    b, s, d = x.shape
    n = b * s
    depth = w_in.shape[0]
    tile = ATTN_TILE
    assert tile % MOBA_BLOCK == 0 and n % 2048 == 0 and (s // MOBA_BLOCK) % 8 == 0
    assert s % (tile * PIPELINE_UNROLL) == 0 and PIPELINE_UNROLL % 2 == 0

    xf = x.reshape(n, d)
    bias = _bias_tiles(rel_bias, tile)

    half = MLA_ROPE_DIM // 2
    inv_freq = ROPE_THETA ** (-jnp.arange(half, dtype=F32) * 2.0 / MLA_ROPE_DIM)
    ang = positions.astype(F32).reshape(n, 1) * inv_freq
    cos, sin = jnp.cos(ang), jnp.sin(ang)
    ct = jnp.concatenate([jnp.ones((n, MLA_NOPE_DIM), F32), cos, cos,
                          jnp.zeros((n, LANES - MLA_NOPE_DIM - MLA_ROPE_DIM), F32)], axis=1)
    st = jnp.concatenate([jnp.zeros((n, MLA_NOPE_DIM), F32), sin, sin,
                          jnp.zeros((n, LANES - MLA_NOPE_DIM - MLA_ROPE_DIM), F32)], axis=1)

    n_diff = DIFF_HEADS * 2 * DIFF_HEAD_DIM
    n_moba = MOBA_HEADS * MOBA_HEAD_DIM
    qkv_scale = jnp.concatenate([
        jnp.full((n_diff,), LOG2E / math.sqrt(DIFF_HEAD_DIM), F32), jnp.ones((2 * n_diff,), F32),
        jnp.full((n_moba,), LOG2E / math.sqrt(MOBA_HEAD_DIM), F32), jnp.ones((2 * n_moba,), F32),
        jnp.ones((N_BRANCH * d,), F32)]).reshape(1, -1)

    row = lambda a: a.reshape(1, -1)
    for l in range(depth):
        lam_init = 0.8 - 0.6 * math.exp(-0.3 * l)
        w_qkvg, w_small, wuq, wukv = _layer_weights(w_in[l], mla_w_uq[l], mla_w_ukv[l])
        nw = row(norm_mix_pre[l])
        qkvg = _norm_matmul(xf, nw, w_qkvg, qkv_scale, BF16, 2048, 1024)

        qkv3 = qkvg.reshape(b, s, -1)
        scal = jnp.full((1,), lam_init, F32)
        oa = _diff_attention(qkv3, bias, scal, diff_lambda[l], row(diff_subln[l]), tile)
        ob = _moba_attention(qkv3, bias, tile)
        mq, mk, mv = _mla_prep(xf, nw, w_small, row(mla_q_norm[l]), row(mla_kv_norm[l]), wuq, wukv, ct, st, 512)
        oc = _mla_attention(mq.reshape(b, s, -1), mk.reshape(b, s, -1), mv.reshape(b, s, -1), tile)

        xf = _merge(xf, oa, ob, oc, qkvg, w_branch[l].astype(BF16), w_out[l].astype(BF16), row(norm_mix_post[l]))
        xf = _mlp(xf, row(norm_mlp_pre[l]), w_up[l].astype(BF16), w_down[l].astype(BF16),
                  row(norm_mlp_post[l]), 1024, 1024)
    return xf.reshape(b, s, d)
```

```python
import functools
import itertools
import math

import jax
import jax.numpy as jnp
from jax import lax
from jax.experimental import pallas as pl
from jax.experimental.pallas import tpu as pltpu

F32 = jnp.float32
BF16 = jnp.bfloat16

RMS_EPS = 1e-6
NEG_INF = -1e30
STAT_INIT = -(2.0 ** 100)

T5_BUCKETS = 32
T5_MAX_EXACT = T5_BUCKETS // 2
T5_MAX_DISTANCE = 1024

DIFF_HEADS = 4
DIFF_HEAD_DIM = 64
MOBA_HEADS = 4
MOBA_HEAD_DIM = 128
MOBA_BLOCK = 256
MOBA_TOPK = 3
MLA_HEADS = 8
MLA_Q_RANK = 256
MLA_KV_RANK = 128
MLA_NOPE_DIM = 64
MLA_ROPE_DIM = 32
MLA_V_DIM = 64
ROPE_THETA = 10000.0
N_BRANCH = 3
BRANCH_WIDTH = 512

LOG2E = math.log2(math.e)
DIFF_HEADS_PER_STEP = 2
MOBA_HEADS_PER_STEP = 2
MLA_HEADS_PER_STEP = 2
PIPELINE_UNROLL = 4
SUM_ROWS = 16
LANES = 128
ATTN_TILE = 512
VMEM_LIMIT = 48 * 1024 * 1024


def _cparams(*sem):
    return pltpu.CompilerParams(dimension_semantics=sem, vmem_limit_bytes=VMEM_LIMIT)


def _rms(x, w):
    return x * lax.rsqrt(jnp.mean(x * x, axis=-1, keepdims=True) + RMS_EPS) * w


def _dot(a, b):
    return jnp.dot(a, b, preferred_element_type=F32)


def _dot_nt(a, b):
    return lax.dot_general(a, b, (((1,), (1,)), ((), ())), preferred_element_type=F32)


def _norm_matmul_kernel(x_ref, nw_ref, w_ref, cs_ref, o_ref, h_ref):
    @pl.when(pl.program_id(1) == 0)
    def _():
        h_ref[...] = _rms(x_ref[...], nw_ref[...]).astype(BF16)

    o_ref[...] = (_dot(h_ref[...], w_ref[...]) * cs_ref[...]).astype(o_ref.dtype)


def _norm_matmul(x, nw, w, colscale, out_dtype, tm, tn):
    n, d = x.shape
    nout = w.shape[1]
    return pl.pallas_call(
        _norm_matmul_kernel,
        grid=(n // tm, nout // tn),
        in_specs=[
            pl.BlockSpec((tm, d), lambda i, j: (i, 0)),
            pl.BlockSpec((1, d), lambda i, j: (0, 0)),
            pl.BlockSpec((d, tn), lambda i, j: (0, j)),
            pl.BlockSpec((1, tn), lambda i, j: (0, j)),
        ],
        out_specs=pl.BlockSpec((tm, tn), lambda i, j: (i, j)),
        out_shape=jax.ShapeDtypeStruct((n, nout), out_dtype),
        scratch_shapes=[pltpu.VMEM((tm, d), BF16)],
        compiler_params=_cparams("parallel", "arbitrary"),
        name="norm_matmul",
    )(x, nw, w, colscale)


def _bias_tile_kernel(tab_ref, o_ref, *, tile):
    h = pl.program_id(0)
    d = pl.program_id(1)
    c = lax.broadcasted_iota(jnp.int32, (tile, tile), 0)
    r = lax.broadcasted_iota(jnp.int32, (tile, tile), 1)
    rel = d * tile + r - c
    n = jnp.maximum(rel, 0)
    nf = jnp.maximum(n, 1).astype(F32)
    large = T5_MAX_EXACT + (jnp.log(nf / T5_MAX_EXACT)
                            / math.log(T5_MAX_DISTANCE / T5_MAX_EXACT)
                            * (T5_BUCKETS - T5_MAX_EXACT)).astype(jnp.int32)
    large = jnp.minimum(large, T5_BUCKETS - 1)
    bucket = jnp.where(n < T5_MAX_EXACT, n, large)
    val = jnp.zeros((tile, tile), F32)
    for b in range(T5_BUCKETS):
        val = jnp.where(bucket == b, tab_ref[b, h], val)
    o_ref[0, 0] = jnp.where(rel >= 0, val * LOG2E, NEG_INF)


def _num_bias_tiles(tile):
    return -(-(T5_MAX_DISTANCE + tile - 1) // tile) + 1


def _bias_tiles(rel_bias, tile):
    nh = rel_bias.shape[1]
    nd = _num_bias_tiles(tile)
    return pl.pallas_call(
        functools.partial(_bias_tile_kernel, tile=tile),
        grid=(nh, nd),
        in_specs=[pl.BlockSpec(memory_space=pltpu.SMEM)],
        out_specs=pl.BlockSpec((1, 1, tile, tile), lambda h, d: (h, d, 0, 0)),
        out_shape=jax.ShapeDtypeStruct((nh, nd, tile, tile), F32),
        compiler_params=_cparams("parallel", "parallel"),
        name="t5_bias_tiles",
    )(rel_bias)


def _store_scores(s, s_ref, mx_ref):
    s_ref[...] = s.astype(BF16)
    mx_ref[...] = jnp.max(s, axis=0, keepdims=True).astype(BF16).astype(F32)


def _softmax_update(s_ref, mx_ref, vt, m_ref, acc_ref):
    m_prev = m_ref[...]
    m_new = jnp.maximum(m_prev, mx_ref[...])
    alpha = jnp.exp2(m_prev - m_new)
    p = jnp.exp2(s_ref[...] - m_new.astype(BF16))
    acc_ref[...] = alpha * acc_ref[...] + _dot(vt, p)
    m_ref[...] = m_new


def _paired_pipeline(i, nq, n_maps, fill, values, consume):
    def item(t):
        second = (jnp.int32(t) > i).astype(jnp.int32)
        j = t - second * (i + 1)
        dist = i + second * (nq - 1 - 2 * i) - j
        return second, j, dist

    def fill_item(t, slot):
        qt, j, dist = item(t)
        for mp in range(n_maps):
            fill(qt, j, dist, slot, mp)

    def step(t, slot):
        qt_f, j_f, dist_f = item(t + 1)
        qt_u, j_u, _ = item(t)
        vts = values(j_u)
        for mp in range(n_maps):
            fill(qt_f, j_f, dist_f, 1 - slot, mp)
            consume(qt_u, vts[mp], slot, mp)

    fill_item(0, 0)

    def trip(r, c):
        for u in range(PIPELINE_UNROLL):
            step(r * PIPELINE_UNROLL + u, u % 2)
        return c

    lax.fori_loop(0, nq // PIPELINE_UNROLL, trip, 0)
    qt, j, _ = item(nq)
    vts = values(j)
    for mp in range(n_maps):
        consume(qt, vts[mp], 0, mp)


def _values_t(v):
    return jnp.concatenate([v.T, jnp.ones((SUM_ROWS, v.shape[0]), v.dtype)], axis=0)


def _init_stats(m_ref, acc_ref):
    m_ref[...] = jnp.full(m_ref.shape, STAT_INIT, F32)
    acc_ref[...] = jnp.zeros(acc_ref.shape, F32)


def _stat_scratch(tile, dv):
    return [pltpu.VMEM((2, 1, tile), F32), pltpu.VMEM((2, dv + SUM_ROWS, tile), F32)]


def _score_scratch(n_maps, tile):
    return [pltpu.VMEM((2, n_maps, tile, tile), BF16), pltpu.VMEM((2, n_maps, 1, tile), F32)]


def _normalized(acc_ref, qt, dv):
    return acc_ref[qt, :dv, :] / acc_ref[qt, dv:dv + 1, :]


def _paired_query_specs(tile, width, nq, col_of):
    return [pl.BlockSpec((1, tile, width), lambda bi, h, i: (bi, i, col_of(h))),
            pl.BlockSpec((1, tile, width), lambda bi, h, i: (bi, nq - 1 - i, col_of(h)))]


def _paired_output(b, s, tile, width, total_width):
    spec = pl.BlockSpec((1, 1, 2, tile, width), lambda bi, h, i: (bi, i, 0, 0, h))
    return spec, jax.ShapeDtypeStruct((b, s // tile // 2, 2, tile, total_width), BF16)


def _pair_order_rows(nq, tile, width):
    def index(r):
        t = lax.rem(r, nq)
        return lax.div(r, nq), jnp.minimum(t, nq - 1 - t), (t >= nq // 2).astype(jnp.int32), 0, 0
    return pl.BlockSpec((1, 1, 1, tile, width), index)


def _key_rows(j, tile):
    return pl.ds(pl.multiple_of(j * tile, tile), tile)


def _diff_kernel(scal_ref, lam_ref, subw_ref, qa_ref, qb_ref, k_ref, v_ref, bias_ref, o_ref,
                 qm_ref, s_ref, mx_ref, *stat_refs, tile, nd, nq, heads):
    i = pl.program_id(2)
    hd = 2 * DIFF_HEAD_DIM
    n_maps = 2 * heads
    for qt, q_ref in enumerate((qa_ref, qb_ref)):
        for hh in range(heads):
            q = q_ref[0, :, hh * hd:(hh + 1) * hd]
            lane = lax.broadcasted_iota(jnp.int32, q.shape, 1)
            zero = jnp.zeros_like(q)
            qm_ref[qt, 2 * hh] = jnp.where(lane < DIFF_HEAD_DIM, q, zero)
            qm_ref[qt, 2 * hh + 1] = jnp.where(lane >= DIFF_HEAD_DIM, q, zero)
    stats = [stat_refs[2 * mp:2 * mp + 2] for mp in range(n_maps)]
    for st in stats:
        _init_stats(*st)

    def fill(qt, j, dist, slot, mp):
        hh = mp // 2
        k = k_ref[0, _key_rows(j, tile), hh * hd:(hh + 1) * hd]
        bias = bias_ref[hh, jnp.minimum(dist, nd - 1)]
        _store_scores(_dot_nt(k, qm_ref[qt, mp]) + bias, s_ref.at[slot, mp], mx_ref.at[slot, mp])

    def values(j):
        rows = _key_rows(j, tile)
        vts = [_values_t(v_ref[0, rows, hh * hd:(hh + 1) * hd]) for hh in range(heads)]
        return [vts[mp // 2] for mp in range(n_maps)]

    def consume(qt, vt, slot, mp):
        m_ref, acc_ref = stats[mp]
        _softmax_update(s_ref.at[slot, mp], mx_ref.at[slot, mp], vt, m_ref.at[qt], acc_ref.at[qt])

    _paired_pipeline(i, nq, n_maps, fill, values, consume)

    lam_init = scal_ref[0]
    lv = lam_ref[...]
    lam = (jnp.exp(jnp.sum(lv[0:1] * lv[1:2], axis=1, keepdims=True))
           - jnp.exp(jnp.sum(lv[2:3] * lv[3:4], axis=1, keepdims=True)) + lam_init)
    for hh, qt in itertools.product(range(heads), range(2)):
        o = (_normalized(stats[2 * hh][1], qt, hd) - lam * _normalized(stats[2 * hh + 1][1], qt, hd)).T
        o_ref[0, 0, qt, :, hh * hd:(hh + 1) * hd] = (
            _rms(o, subw_ref[...]) * (1.0 - lam_init)).astype(o_ref.dtype)


def _diff_attention(qkv, bias, scal, lam_vecs, subw, tile):
    b, s, _ = qkv.shape
    nd = bias.shape[1]
    nq = s // tile
    hd = 2 * DIFF_HEAD_DIM
    hg = DIFF_HEADS_PER_STEP
    w = hg * hd
    groups = DIFF_HEADS // hg
    out_spec, out_shape = _paired_output(b, s, tile, w, DIFF_HEADS * hd)
    return pl.pallas_call(
        functools.partial(_diff_kernel, tile=tile, nd=nd, nq=nq, heads=hg),
        grid=(b, groups, nq // 2),
        in_specs=[
            pl.BlockSpec(memory_space=pltpu.SMEM),
            pl.BlockSpec(lam_vecs.shape, lambda bi, h, i: (0, 0)),
            pl.BlockSpec((1, hd), lambda bi, h, i: (0, 0)),
            *_paired_query_specs(tile, w, nq, lambda h: h),
            pl.BlockSpec((1, s, w), lambda bi, h, i: (bi, 0, groups + h)),
            pl.BlockSpec((1, s, w), lambda bi, h, i: (bi, 0, 2 * groups + h)),
            pl.BlockSpec((hg, nd, tile, tile), lambda bi, h, i: (h, 0, 0, 0)),
        ],
        out_specs=out_spec,
        out_shape=out_shape,
        scratch_shapes=[pltpu.VMEM((2, 2 * hg, tile, hd), BF16)] + _score_scratch(2 * hg, tile)
        + 2 * hg * _stat_scratch(tile, hd),
        compiler_params=_cparams("parallel", "parallel", "arbitrary"),
        name="diff_attention",
    )(scal, lam_vecs, subw, qkv, qkv, qkv, qkv, bias)


def _moba_kernel(qa_ref, qb_ref, k_ref, oh_ref, v_ref, bias_ref, o_ref,
                 kmean_ref, qaug_ref, s_ref, mx_ref, *stat_refs, tile, nd, nblk, heads, nq):
    blk = MOBA_BLOCK
    hd = MOBA_HEAD_DIM
    per_tile = tile // blk
    i = pl.program_id(2)
    stats = [stat_refs[2 * hh:2 * hh + 2] for hh in range(heads)]

    @pl.when(i == 0)
    def _():
        for hh in range(heads):
            for n in range(nblk):
                kb = k_ref[0, n * blk:(n + 1) * blk, hh * hd:(hh + 1) * hd].astype(F32)
                mean = jnp.mean(kb, axis=0, keepdims=True)
                for part in range(3):
                    term = mean.astype(BF16).astype(F32)
                    kmean_ref[hh, part * nblk + n:part * nblk + n + 1, :] = term
                    mean = mean - term

    for hh, (qt, q_ref) in itertools.product(range(heads), enumerate((qa_ref, qb_ref))):
        q = q_ref[0, :, hh * hd:(hh + 1) * hd]
        q_tile = i if qt == 0 else nq - 1 - i
        parts = _dot_nt(kmean_ref[hh].astype(BF16), q)
        gate = parts[:nblk] + parts[nblk:2 * nblk] + parts[2 * nblk:]
        blk_id = lax.broadcasted_iota(jnp.int32, gate.shape, 0)
        blk_f = blk_id.astype(F32)
        lane = lax.broadcasted_iota(jnp.int32, gate.shape, 1)
        own = q_tile * per_tile + lax.shift_right_logical(lane, int(math.log2(blk)))
        valid = blk_id < own
        gate = jnp.where(valid, gate, NEG_INF)
        sel = blk_id == own
        for _ in range(MOBA_TOPK):
            top = jnp.max(gate, axis=0, keepdims=True)
            first = jnp.min(jnp.where(gate == top, blk_f, float(nblk)), axis=0, keepdims=True)
            hit = blk_f == first
            sel = jnp.logical_or(sel, jnp.logical_and(hit, valid))
            gate = jnp.where(hit, -3.0e38, gate)
        pen = jnp.where(sel, 0.0, NEG_INF)
        pen = jnp.concatenate([pen, jnp.zeros((LANES - nblk, tile), F32)], axis=0)
        qaug_ref[qt, hh, :, :LANES] = q
        qaug_ref[qt, hh, :, LANES:] = pen.T.astype(BF16)
    for st in stats:
        _init_stats(*st)

    def fill(qt, j, dist, slot, hh):
        rows = _key_rows(j, tile)
        kaug = jnp.concatenate([k_ref[0, rows, hh * hd:(hh + 1) * hd], oh_ref[rows, :]], axis=1)
        bias = bias_ref[hh, jnp.minimum(dist, nd - 1)]
        _store_scores(_dot_nt(kaug, qaug_ref[qt, hh]) + bias, s_ref.at[slot, hh], mx_ref.at[slot, hh])

    def values(j):
        rows = _key_rows(j, tile)
        return [_values_t(v_ref[0, rows, hh * hd:(hh + 1) * hd]) for hh in range(heads)]

    def consume(qt, vt, slot, hh):
        m_ref, acc_ref = stats[hh]
        _softmax_update(s_ref.at[slot, hh], mx_ref.at[slot, hh], vt, m_ref.at[qt], acc_ref.at[qt])

    _paired_pipeline(i, nq, heads, fill, values, consume)
    for hh, qt in itertools.product(range(heads), range(2)):
        o_ref[0, 0, qt, :, hh * hd:(hh + 1) * hd] = _normalized(stats[hh][1], qt, hd).T.astype(o_ref.dtype)


def _moba_attention(qkv, bias, tile):
    b, s, _ = qkv.shape
    nd = bias.shape[1]
    nblk = s // MOBA_BLOCK
    hd = MOBA_HEAD_DIM
    hg = MOBA_HEADS_PER_STEP
    w = hg * hd
    col0 = 3 * DIFF_HEADS // hg
    groups = MOBA_HEADS // hg
    onehot = (jnp.arange(s)[:, None] // MOBA_BLOCK == jnp.arange(LANES)[None, :]).astype(BF16)
    nq = s // tile
    out_spec, out_shape = _paired_output(b, s, tile, w, MOBA_HEADS * hd)
    return pl.pallas_call(
        functools.partial(_moba_kernel, tile=tile, nd=nd, nblk=nblk, heads=hg, nq=nq),
        grid=(b, groups, nq // 2),
        in_specs=[
            *_paired_query_specs(tile, w, nq, lambda h: col0 + h),
            pl.BlockSpec((1, s, w), lambda bi, h, i: (bi, 0, col0 + groups + h)),
            pl.BlockSpec((s, LANES), lambda bi, h, i: (0, 0)),
            pl.BlockSpec((1, s, w), lambda bi, h, i: (bi, 0, col0 + 2 * groups + h)),
            pl.BlockSpec((hg, nd, tile, tile), lambda bi, h, i: (DIFF_HEADS // hg + h, 0, 0, 0)),
        ],
        out_specs=out_spec,
        out_shape=out_shape,
        scratch_shapes=[pltpu.VMEM((hg, 3 * nblk, hd), F32), pltpu.VMEM((2, hg, tile, 2 * LANES), BF16)]
        + _score_scratch(hg, tile) + hg * _stat_scratch(tile, hd),
        compiler_params=_cparams("parallel", "parallel", "arbitrary"),
        name="moba_attention",
    )(qkv, qkv, qkv, onehot, qkv, bias)


def _mla_prep_kernel(x_ref, nw_ref, wsmall_ref, qnw_ref, kvnw_ref, wuq_ref, wukv_ref, ct_ref, st_ref,
                     q_ref, k_ref, v_ref, *, scale):
    kw = MLA_HEADS * LANES
    ct = ct_ref[...]
    st = st_ref[...]
    small = _dot(_rms(x_ref[...], nw_ref[...]).astype(BF16), wsmall_ref[...])
    cq = small[:, :MLA_Q_RANK]
    ckv = small[:, MLA_Q_RANK:MLA_Q_RANK + MLA_KV_RANK]
    kpe = small[:, MLA_Q_RANK + MLA_KV_RANK:MLA_Q_RANK + MLA_KV_RANK + LANES]
    kpe_rot = small[:, MLA_Q_RANK + MLA_KV_RANK + LANES:]
    qq = _dot(_rms(cq, qnw_ref[...]).astype(BF16), wuq_ref[...])
    kk = _dot(_rms(ckv, kvnw_ref[...]).astype(BF16), wukv_ref[...])
    k_rope = kpe * ct + kpe_rot * st
    for h in range(MLA_HEADS):
        cols = slice(h * LANES, (h + 1) * LANES)
        rot = qq[:, kw + h * LANES:kw + (h + 1) * LANES]
        q_ref[:, cols] = ((qq[:, cols] * ct + rot * st) * scale).astype(BF16)
        k_ref[:, cols] = (kk[:, cols] + k_rope).astype(BF16)
    v_ref[...] = kk[:, kw:].astype(BF16)


def _mla_prep(x, nw, wsmall, qnw, kvnw, wuq, wukv, ct, st, tm):
    n, d = x.shape
    kw = MLA_HEADS * LANES
    vw = MLA_HEADS * MLA_V_DIM
    scale = LOG2E / math.sqrt(MLA_NOPE_DIM + MLA_ROPE_DIM)
    row = lambda i: (i, 0)
    const = lambda i: (0, 0)
    return pl.pallas_call(
        functools.partial(_mla_prep_kernel, scale=scale),
        grid=(n // tm,),
        in_specs=[
            pl.BlockSpec((tm, d), row),
            pl.BlockSpec(nw.shape, const),
            pl.BlockSpec(wsmall.shape, const),
            pl.BlockSpec(qnw.shape, const),
            pl.BlockSpec(kvnw.shape, const),
            pl.BlockSpec(wuq.shape, const),
            pl.BlockSpec(wukv.shape, const),
            pl.BlockSpec((tm, LANES), row),
            pl.BlockSpec((tm, LANES), row),
        ],
        out_specs=[pl.BlockSpec((tm, kw), row), pl.BlockSpec((tm, kw), row), pl.BlockSpec((tm, vw), row)],
        out_shape=[jax.ShapeDtypeStruct((n, kw), BF16), jax.ShapeDtypeStruct((n, kw), BF16),
                   jax.ShapeDtypeStruct((n, vw), BF16)],
        compiler_params=_cparams("parallel"),
        name="mla_prep",
    )(x, nw, wsmall, qnw, kvnw, wuq, wukv, ct, st)


def _mla_kernel(qa_ref, qb_ref, k_ref, v_ref, o_ref, q_ref, mask_ref, s_ref, mx_ref, *stat_refs,
                tile, heads, nq):
    i = pl.program_id(2)
    stats = [stat_refs[2 * hh:2 * hh + 2] for hh in range(heads)]
    for st in stats:
        _init_stats(*st)
    q_ref[0] = qa_ref[0]
    q_ref[1] = qb_ref[0]
    key = lax.broadcasted_iota(jnp.int32, (tile, tile), 0)
    qry = lax.broadcasted_iota(jnp.int32, (tile, tile), 1)
    mask_ref[0] = jnp.where(qry >= key, 0.0, NEG_INF)
    mask_ref[1] = jnp.zeros((tile, tile), F32)

    def fill(qt, j, dist, slot, hh):
        rows = _key_rows(j, tile)
        head = slice(hh * LANES, (hh + 1) * LANES)
        mask = mask_ref[jnp.minimum(dist, 1)]
        _store_scores(_dot_nt(k_ref[0, rows, head], q_ref[qt, :, head]) + mask,
                      s_ref.at[slot, hh], mx_ref.at[slot, hh])

    def values(j):
        rows = _key_rows(j, tile)
        ones = jnp.ones((SUM_ROWS, tile), BF16)
        out = []
        for pr in range(heads // 2):
            vt = v_ref[0, rows, pr * LANES:(pr + 1) * LANES].T
            out += [jnp.concatenate([vt[:MLA_V_DIM], ones], axis=0), jnp.concatenate([vt[MLA_V_DIM:], ones], axis=0)]
        return out

    def consume(qt, vt, slot, hh):
        m_ref, acc_ref = stats[hh]
        _softmax_update(s_ref.at[slot, hh], mx_ref.at[slot, hh], vt, m_ref.at[qt], acc_ref.at[qt])

    _paired_pipeline(i, nq, heads, fill, values, consume)
    for pr, qt in itertools.product(range(heads // 2), range(2)):
        o = jnp.concatenate([_normalized(stats[2 * pr][1], qt, MLA_V_DIM),
                             _normalized(stats[2 * pr + 1][1], qt, MLA_V_DIM)], axis=0)
        o_ref[0, 0, qt, :, pr * LANES:(pr + 1) * LANES] = o.T.astype(o_ref.dtype)


def _mla_attention(q, k, v, tile):
    b, s, _ = q.shape
    hg = MLA_HEADS_PER_STEP
    groups = MLA_HEADS // hg
    nq = s // tile
    out_spec, out_shape = _paired_output(b, s, tile, hg * MLA_V_DIM, MLA_HEADS * MLA_V_DIM)
    return pl.pallas_call(
        functools.partial(_mla_kernel, tile=tile, heads=hg, nq=nq),
        grid=(b, groups, nq // 2),
        in_specs=[
            *_paired_query_specs(tile, hg * LANES, nq, lambda h: h),
            pl.BlockSpec((1, s, hg * LANES), lambda bi, h, i: (bi, 0, h)),
            pl.BlockSpec((1, s, hg * MLA_V_DIM), lambda bi, h, i: (bi, 0, h)),
        ],
        out_specs=out_spec,
        out_shape=out_shape,
        scratch_shapes=[pltpu.VMEM((2, tile, hg * LANES), BF16), pltpu.VMEM((2, tile, tile), F32)]
        + _score_scratch(hg, tile) + hg * _stat_scratch(tile, MLA_V_DIM),
        compiler_params=_cparams("parallel", "parallel", "arbitrary"),
        name="mla_attention",
    )(q, q, k, v)


def _merge_kernel(x_ref, oa_ref, ob_ref, oc_ref, g_ref, wb_ref, wo_ref, nw_ref, o_ref):
    d = x_ref.shape[1]
    mixed = jnp.zeros(x_ref.shape, F32)
    for gi, br_ref in enumerate((oa_ref, ob_ref, oc_ref)):
        br = _dot(br_ref[0, 0, 0], wb_ref[gi])
        gate = 1.0 / (1.0 + jnp.exp(-g_ref[:, gi * d:(gi + 1) * d].astype(F32)))
        mixed = mixed + gate * br
    y = _dot(mixed.astype(BF16), wo_ref[...])
    o_ref[...] = x_ref[...] + _rms(y, nw_ref[...])


def _merge(x, oa, ob, oc, qkvg, wb, wo, nw):
    n, d = x.shape
    _, half, _, tm, _ = oa.shape
    branch = _pair_order_rows(2 * half, tm, BRANCH_WIDTH)
    g_block = qkvg.shape[1] // (N_BRANCH * d) - 1
    row = lambda i: (i, 0)
    return pl.pallas_call(
        _merge_kernel,
        grid=(n // tm,),
        in_specs=[
            pl.BlockSpec((tm, d), row),
            branch, branch, branch,
            pl.BlockSpec((tm, N_BRANCH * d), lambda i: (i, g_block)),
            pl.BlockSpec(wb.shape, lambda i: (0, 0, 0)),
            pl.BlockSpec(wo.shape, lambda i: (0, 0)),
            pl.BlockSpec((1, d), lambda i: (0, 0)),
        ],
        out_specs=pl.BlockSpec((tm, d), row),
        out_shape=jax.ShapeDtypeStruct((n, d), F32),
        compiler_params=_cparams("parallel"),
        name="branch_merge",
    )(x, oa, ob, oc, qkvg, wb, wo, nw)


def _mlp_kernel(x_ref, nw1_ref, wu_ref, wd_ref, nw2_ref, o_ref, h_ref, acc_ref):
    j = pl.program_id(1)

    @pl.when(j == 0)
    def _():
        h_ref[...] = _rms(x_ref[...], nw1_ref[...]).astype(BF16)
        acc_ref[...] = jnp.zeros(acc_ref.shape, F32)

    u = jnp.square(jnp.maximum(_dot(h_ref[...], wu_ref[...]), 0.0))
    acc_ref[...] += _dot(u.astype(BF16), wd_ref[...])

    @pl.when(j == pl.num_programs(1) - 1)
    def _():
        o_ref[...] = x_ref[...] + _rms(acc_ref[...], nw2_ref[...])


def _mlp(x, nw1, wu, wd, nw2, tm, tf):
    n, d = x.shape
    f = wu.shape[1]
    return pl.pallas_call(
        _mlp_kernel,
        grid=(n // tm, f // tf),
        in_specs=[
            pl.BlockSpec((tm, d), lambda i, j: (i, 0)),
            pl.BlockSpec((1, d), lambda i, j: (0, 0)),
            pl.BlockSpec((d, tf), lambda i, j: (0, j)),
            pl.BlockSpec((tf, d), lambda i, j: (j, 0)),
            pl.BlockSpec((1, d), lambda i, j: (0, 0)),
        ],
        out_specs=pl.BlockSpec((tm, d), lambda i, j: (i, 0)),
        out_shape=jax.ShapeDtypeStruct((n, d), F32),
        scratch_shapes=[pltpu.VMEM((tm, d), BF16), pltpu.VMEM((tm, d), F32)],
        compiler_params=_cparams("parallel", "arbitrary"),
        name="relu2_mlp",
    )(x, nw1, wu, wd, nw2)


def _rotate_half_cols(w):
    half = w.shape[-1] // 2
    return jnp.concatenate([-w[..., half:], w[..., :half]], axis=-1)


def _pad_cols(w, before, total):
    pad = [(0, 0)] * (w.ndim - 1) + [(before, total - before - w.shape[-1])]
    return jnp.pad(w, pad)


def _layer_weights(w_in, w_uq, w_ukv):
    n_qkv = 3 * DIFF_HEADS * 2 * DIFF_HEAD_DIM + 3 * MOBA_HEADS * MOBA_HEAD_DIM
    o_cq = n_qkv
    o_ckv = o_cq + MLA_Q_RANK
    o_kpe = o_ckv + MLA_KV_RANK
    o_g = o_kpe + MLA_ROPE_DIM
    w_qkvg = jnp.concatenate([w_in[:, :n_qkv], w_in[:, o_g:]], axis=1).astype(BF16)
    w_kpe = w_in[:, o_kpe:o_g]
    w_small = jnp.concatenate([
        w_in[:, o_cq:o_kpe],
        _pad_cols(w_kpe, MLA_NOPE_DIM, LANES),
        _pad_cols(_rotate_half_cols(w_kpe), MLA_NOPE_DIM, LANES)], axis=1).astype(BF16)

    qk_dim = MLA_NOPE_DIM + MLA_ROPE_DIM
    uq = w_uq.reshape(MLA_Q_RANK, MLA_HEADS, qk_dim)
    uq_plain = _pad_cols(uq, 0, LANES).reshape(MLA_Q_RANK, MLA_HEADS * LANES)
    uq_rot = _pad_cols(_rotate_half_cols(uq[..., MLA_NOPE_DIM:]), MLA_NOPE_DIM, LANES)
    wuq = jnp.concatenate([uq_plain, uq_rot.reshape(MLA_Q_RANK, MLA_HEADS * LANES)], axis=1).astype(BF16)

    ukv = w_ukv.reshape(MLA_KV_RANK, MLA_HEADS, MLA_NOPE_DIM + MLA_V_DIM)
    uk = _pad_cols(ukv[..., :MLA_NOPE_DIM], 0, LANES).reshape(MLA_KV_RANK, MLA_HEADS * LANES)
    uv = ukv[..., MLA_NOPE_DIM:].reshape(MLA_KV_RANK, MLA_HEADS * MLA_V_DIM)
    wukv = jnp.concatenate([uk, uv], axis=1).astype(BF16)
    return w_qkvg, w_small, wuq, wukv


def kernel(x, positions, rel_bias, norm_mix_pre, norm_mix_post, norm_mlp_pre, norm_mlp_post, w_in, diff_lambda, diff_subln, mla_q_norm, mla_w_uq, mla_kv_norm, mla_w_ukv, w_branch, w_out, w_up, w_down):
    b, s, d = x.shape
    n = b * s
    depth = w_in.shape[0]
    tile = ATTN_TILE
    assert tile % MOBA_BLOCK == 0 and n % 2048 == 0 and (s // MOBA_BLOCK) % 8 == 0
    assert s % (tile * PIPELINE_UNROLL) == 0 and PIPELINE_UNROLL % 2 == 0

    xf = x.reshape(n, d)
    bias = _bias_tiles(rel_bias, tile)

    half = MLA_ROPE_DIM // 2
    inv_freq = ROPE_THETA ** (-jnp.arange(half, dtype=F32) * 2.0 / MLA_ROPE_DIM)
    ang = positions.astype(F32).reshape(n, 1) * inv_freq
    cos, sin = jnp.cos(ang), jnp.sin(ang)
    ct = jnp.concatenate([jnp.ones((n, MLA_NOPE_DIM), F32), cos, cos,
                          jnp.zeros((n, LANES - MLA_NOPE_DIM - MLA_ROPE_DIM), F32)], axis=1)
    st = jnp.concatenate([jnp.zeros((n, MLA_NOPE_DIM), F32), sin, sin,
                          jnp.zeros((n, LANES - MLA_NOPE_DIM - MLA_ROPE_DIM), F32)], axis=1)

    n_diff = DIFF_HEADS * 2 * DIFF_HEAD_DIM
    n_moba = MOBA_HEADS * MOBA_HEAD_DIM
    qkv_scale = jnp.concatenate([
        jnp.full((n_diff,), LOG2E / math.sqrt(DIFF_HEAD_DIM), F32), jnp.ones((2 * n_diff,), F32),
        jnp.full((n_moba,), LOG2E / math.sqrt(MOBA_HEAD_DIM), F32), jnp.ones((2 * n_moba,), F32),
        jnp.ones((N_BRANCH * d,), F32)]).reshape(1, -1)

    row = lambda a: a.reshape(1, -1)
    for l in range(depth):
        lam_init = 0.8 - 0.6 * math.exp(-0.3 * l)
        w_qkvg, w_small, wuq, wukv = _layer_weights(w_in[l], mla_w_uq[l], mla_w_ukv[l])
        nw = row(norm_mix_pre[l])
        qkvg = _norm_matmul(xf, nw, w_qkvg, qkv_scale, BF16, 2048, 1024)

        qkv3 = qkvg.reshape(b, s, -1)
        scal = jnp.full((1,), lam_init, F32)
        oa = _diff_attention(qkv3, bias, scal, diff_lambda[l], row(diff_subln[l]), tile)
        ob = _moba_attention(qkv3, bias, tile)
        mq, mk, mv = _mla_prep(xf, nw, w_small, row(mla_q_norm[l]), row(mla_kv_norm[l]), wuq, wukv, ct, st, 512)
        oc = _mla_attention(mq.reshape(b, s, -1), mk.reshape(b, s, -1), mv.reshape(b, s, -1), tile)

        xf = _merge(xf, oa, ob, oc, qkvg, w_branch[l].astype(BF16), w_out[l].astype(BF16), row(norm_mix_post[l]))
        xf = _mlp(xf, row(norm_mlp_pre[l]), w_up[l].astype(BF16), w_down[l].astype(BF16),
                  row(norm_mlp_post[l]), 1024, 1024)
    return xf.reshape(b, s, d)
```
